```python
import math
import jax
import jax.numpy as jnp
from jax import lax
import numpy as np


D_MODEL = 2048
BATCH = 16
SEQ = 2048
DEPTH = 2

N_EVEN = (DEPTH + 1) // 2
N_ODD = DEPTH // 2
RMS_EPS = 1e-6

CONV_CH = D_MODEL // 2
CONV_K = 3
SB_HEADS = 8
SB_HEAD_DIM = D_MODEL // 16
SB_WIDTH = SB_HEADS * SB_HEAD_DIM
SB_BLOCK = 128
EVEN_IN = 3 * CONV_CH + 3 * SB_WIDTH
EVEN_MIX = CONV_CH + SB_WIDTH

SSD_D_INNER = 2 * D_MODEL
SSD_HEAD_DIM = 64
SSD_HEADS = SSD_D_INNER // SSD_HEAD_DIM
SSD_GROUPS = 8
SSD_HPG = SSD_HEADS // SSD_GROUPS
SSD_STATE = 128
SSD_CONV_K = 4
SSD_CHUNK = 128
SSD_CONV_DIM = SSD_D_INNER + 2 * SSD_GROUPS * SSD_STATE
SSD_IN = SSD_D_INNER + SSD_CONV_DIM + SSD_HEADS
DT_MIN = 0.001
DT_MAX = 0.1

PEER_HEADS = 8
PEER_N_KEYS = 128
PEER_N_EXPERTS = PEER_N_KEYS ** 2
PEER_HALF = 128
PEER_QUERY = 2 * PEER_HALF
PEER_TOPK = 16
PEER_CHUNK = 128

kernel_name = 'hybrid_shortconv_stickbreak_ssd_peer'


def rmsnorm(x, w):
    xf = x.astype(jnp.float32)
    xf = xf * lax.rsqrt(jnp.mean(xf * xf, axis=-1, keepdims=True) + RMS_EPS)
    return xf.astype(x.dtype) * w


def causal_dwconv(u, w):
    k_w = w.shape[0]
    s = u.shape[1]
    up = jnp.pad(u, ((0, 0), (k_w - 1, 0), (0, 0)))
    out = up[:, 0:s] * w[0]
    for k in range(1, k_w):
        out = out + up[:, k:k + s] * w[k]
    return out


def stick_breaking_attention(q, k, v):
    seq = q.shape[2]
    scale = SB_HEAD_DIM ** -0.5
    outs = []
    for blk in range(seq // SB_BLOCK):
        start = blk * SB_BLOCK
        end = start + SB_BLOCK
        qb = q[:, :, start:end]
        kb = k[:, :, :end]
        vb = v[:, :, :end]
        z = jnp.einsum('bhqd,bhkd->bhqk', qb.astype(jnp.float32), kb.astype(jnp.float32)) * scale
        t_pos = start + jnp.arange(SB_BLOCK)
        s_pos = jnp.arange(end)
        mask = s_pos[None, :] < t_pos[:, None]
        log_stay = jnp.where(mask, jax.nn.log_sigmoid(-z), 0.0)
        after = lax.cumsum(log_stay, axis=3, reverse=True) - log_stay
        a = jnp.where(mask, jnp.exp(jax.nn.log_sigmoid(z) + after), 0.0)
        outs.append(jnp.einsum('bhqk,bhkd->bhqd', a.astype(vb.dtype), vb))
    return jnp.concatenate(outs, axis=2)


def even_mixer(xn, w_in, conv_w, w_out):
    b_, s_, _ = xn.shape
    proj = xn @ w_in
    g_b, g_c, xv, q, k, v = jnp.split(
        proj, [CONV_CH, 2 * CONV_CH, 3 * CONV_CH, 3 * CONV_CH + SB_WIDTH,
               3 * CONV_CH + 2 * SB_WIDTH], axis=-1)
    y_conv = g_b * causal_dwconv(g_c * xv, conv_w)
    heads = lambda t: t.reshape(b_, s_, SB_HEADS, SB_HEAD_DIM).transpose(0, 2, 1, 3)
    y_sb = stick_breaking_attention(heads(q), heads(k), heads(v))
    y_sb = y_sb.transpose(0, 2, 1, 3).reshape(b_, s_, SB_WIDTH)
    return jnp.concatenate([y_conv, y_sb], axis=-1) @ w_out


def ssd_mixer(xn, w_in, conv_w, conv_b, dt_bias, a_log, d_skip, norm_w, w_out):
    b_, s_, _ = xn.shape
    f32 = jnp.float32
    proj = xn @ w_in
    z, xbc, dt_raw = jnp.split(proj, [SSD_D_INNER, SSD_D_INNER + SSD_CONV_DIM], axis=-1)
    xbc = jax.nn.silu(causal_dwconv(xbc, conv_w) + conv_b)
    xs, b_in, c_in = jnp.split(xbc, [SSD_D_INNER, SSD_D_INNER + SSD_GROUPS * SSD_STATE], axis=-1)
    dt = jax.nn.softplus(dt_raw.astype(f32) + dt_bias.astype(f32))
    a = -jnp.exp(a_log.astype(f32)).reshape(SSD_GROUPS, SSD_HPG)
    nc = s_ // SSD_CHUNK

    def to_chunks(t, tail):
        return jnp.moveaxis(t.astype(f32).reshape((b_, nc, SSD_CHUNK) + tail), 1, 0)

    x_c = to_chunks(xs, (SSD_GROUPS, SSD_HPG, SSD_HEAD_DIM))
    dt_c = to_chunks(dt, (SSD_GROUPS, SSD_HPG))
    b_c = to_chunks(b_in, (SSD_GROUPS, SSD_STATE))
    c_c = to_chunks(c_in, (SSD_GROUPS, SSD_STATE))
    causal = jnp.tril(jnp.ones((SSD_CHUNK, SSD_CHUNK), dtype=bool))[None, :, :, None, None]

    def step(state, inp):
        xq, dq, bq, cq = inp
        acum = jnp.cumsum(dq * a, axis=1)
        seg = acum[:, :, None] - acum[:, None, :]
        decay = jnp.exp(jnp.where(causal, seg, -jnp.inf))
        cb = jnp.einsum('btgn,bsgn->btsg', cq, bq)
        m = cb[..., None] * decay * dq[:, None]
        y_intra = jnp.einsum('btsgh,bsghp->btghp', m, xq)
        y_inter = jnp.einsum('btgn,bghpn->btghp', cq, state) * jnp.exp(acum)[..., None]
        w = jnp.exp(acum[:, -1:] - acum) * dq
        new_state = (state * jnp.exp(acum[:, -1])[..., None, None]
                     + jnp.einsum('bsgn,bsgh,bsghp->bghpn', bq, w, xq))
        return new_state, y_intra + y_inter

    state0 = jnp.zeros((b_, SSD_GROUPS, SSD_HPG, SSD_HEAD_DIM, SSD_STATE), f32)
    _, y = lax.scan(step, state0, (x_c, dt_c, b_c, c_c))
    y = jnp.moveaxis(y, 0, 1).reshape(b_, s_, SSD_HEADS, SSD_HEAD_DIM)
    y = y + d_skip.astype(f32)[:, None] * xs.astype(f32).reshape(b_, s_, SSD_HEADS, SSD_HEAD_DIM)
    y = y.reshape(b_, s_, SSD_D_INNER) * jax.nn.silu(z.astype(f32))
    yg = y.reshape(b_, s_, SSD_GROUPS, SSD_D_INNER // SSD_GROUPS)
    yg = yg * lax.rsqrt(jnp.mean(yg * yg, axis=-1, keepdims=True) + RMS_EPS)
    y = yg.reshape(b_, s_, SSD_D_INNER).astype(xn.dtype) * norm_w
    return y @ w_out


def peer_ffn(xn, w_q, sub_keys, u, v):
    b_, s_, d_ = xn.shape
    q = (xn @ w_q).reshape(b_, s_, PEER_HEADS, 2, PEER_HALF)
    scores = jnp.einsum('bshcd,hcnd->bshcn', q.astype(jnp.float32), sub_keys.astype(jnp.float32))
    top_vals, top_idx = lax.top_k(scores, PEER_TOPK)
    cand = top_vals[..., 0, :, None] + top_vals[..., 1, None, :]
    cand = cand.reshape(b_, s_, PEER_HEADS, PEER_TOPK * PEER_TOPK)
    best, pos = lax.top_k(cand, PEER_TOPK)
    e1 = jnp.take_along_axis(top_idx[..., 0, :], pos // PEER_TOPK, axis=-1)
    e2 = jnp.take_along_axis(top_idx[..., 1, :], pos % PEER_TOPK, axis=-1)
    experts = e1 * PEER_N_KEYS + e2
    gate = jax.nn.softmax(best, axis=-1).astype(xn.dtype)
    n_chunks = (b_ * s_) // PEER_CHUNK
    x_ch = xn.reshape(n_chunks, PEER_CHUNK, d_)
    e_ch = experts.reshape(n_chunks, PEER_CHUNK, PEER_HEADS * PEER_TOPK)
    g_ch = gate.reshape(n_chunks, PEER_CHUNK, PEER_HEADS * PEER_TOPK)

    def chunk(args):
        xb, eb, gb = args
        h = jnp.einsum('cd,ckd->ck', xb, u[eb])
        act = gb * jax.nn.gelu(h, approximate=False)
        return jnp.einsum('ck,ckd->cd', act, v[eb])

    out = lax.map(chunk, (x_ch, e_ch, g_ch))
    return out.reshape(b_, s_, d_)


def setup_inputs(seed: int = 0):
    key = jax.random.key(seed)
    ks = jax.random.split(key, 20)
    f32 = jnp.float32
    nrm = lambda k, shape, scale: jax.random.normal(k, shape, f32) * scale
    x = nrm(ks[0], (BATCH, SEQ, D_MODEL), 1.0)
    norm_mix = 1.0 + nrm(ks[1], (DEPTH, D_MODEL), 0.02)
    norm_ffn = 1.0 + nrm(ks[2], (DEPTH, D_MODEL), 0.02)
    norm_final = 1.0 + nrm(ks[3], (D_MODEL,), 0.02)
    ev_w_in = nrm(ks[4], (N_EVEN, D_MODEL, EVEN_IN), D_MODEL ** -0.5)
    ev_conv_w = nrm(ks[5], (N_EVEN, CONV_K, CONV_CH), CONV_K ** -0.5)
    ev_w_out = nrm(ks[6], (N_EVEN, EVEN_MIX, D_MODEL), EVEN_MIX ** -0.5)
    ssd_w_in = nrm(ks[7], (N_ODD, D_MODEL, SSD_IN), D_MODEL ** -0.5)
    ssd_conv_w = nrm(ks[8], (N_ODD, SSD_CONV_K, SSD_CONV_DIM), SSD_CONV_K ** -0.5)
    ssd_conv_b = nrm(ks[9], (N_ODD, SSD_CONV_DIM), 0.01)
    uni = jax.random.uniform(ks[10], (N_ODD, SSD_HEADS), f32)
    dt0 = jnp.exp(uni * (math.log(DT_MAX) - math.log(DT_MIN)) + math.log(DT_MIN))
    ssd_dt_bias = dt0 + jnp.log(-jnp.expm1(-dt0))
    ssd_a_log = jnp.log(jax.random.uniform(ks[11], (N_ODD, SSD_HEADS), f32, 1.0, 16.0))
    ssd_d = 1.0 + nrm(ks[12], (N_ODD, SSD_HEADS), 0.02)
    ssd_norm = 1.0 + nrm(ks[13], (N_ODD, SSD_D_INNER), 0.02)
    ssd_w_out = nrm(ks[14], (N_ODD, SSD_D_INNER, D_MODEL), SSD_D_INNER ** -0.5)
    peer_w_q = nrm(ks[15], (DEPTH, D_MODEL, PEER_HEADS * PEER_QUERY), D_MODEL ** -0.5)
    peer_sub_keys = nrm(ks[16], (DEPTH, PEER_HEADS, 2, PEER_N_KEYS, PEER_HALF), PEER_HALF ** -0.5)
    peer_u = nrm(ks[17], (DEPTH, PEER_N_EXPERTS, D_MODEL), D_MODEL ** -0.5)
    peer_v = nrm(ks[18], (DEPTH, PEER_N_EXPERTS, D_MODEL), PEER_HEADS ** -0.5)
    return {'x': x, 'norm_mix': norm_mix, 'norm_ffn': norm_ffn, 'norm_final': norm_final,
            'ev_w_in': ev_w_in, 'ev_conv_w': ev_conv_w, 'ev_w_out': ev_w_out,
            'ssd_w_in': ssd_w_in, 'ssd_conv_w': ssd_conv_w, 'ssd_conv_b': ssd_conv_b,
            'ssd_dt_bias': ssd_dt_bias, 'ssd_a_log': ssd_a_log, 'ssd_d': ssd_d,
            'ssd_norm': ssd_norm, 'ssd_w_out': ssd_w_out,
            'peer_w_q': peer_w_q, 'peer_sub_keys': peer_sub_keys, 'peer_u': peer_u, 'peer_v': peer_v}


def reference(x, norm_mix, norm_ffn, norm_final, ev_w_in, ev_conv_w, ev_w_out,
              ssd_w_in, ssd_conv_w, ssd_conv_b, ssd_dt_bias, ssd_a_log, ssd_d,
              ssd_norm, ssd_w_out, peer_w_q, peer_sub_keys, peer_u, peer_v):
    for i in range(DEPTH):
        j = i // 2
        h = rmsnorm(x, norm_mix[i])
        if i % 2 == 0:
            x = x + even_mixer(h, ev_w_in[j], ev_conv_w[j], ev_w_out[j])
        else:
            x = x + ssd_mixer(h, ssd_w_in[j], ssd_conv_w[j], ssd_conv_b[j], ssd_dt_bias[j],
                              ssd_a_log[j], ssd_d[j], ssd_norm[j], ssd_w_out[j])
        x = x + peer_ffn(rmsnorm(x, norm_ffn[i]), peer_w_q[i], peer_sub_keys[i],
                         peer_u[i], peer_v[i])
    return rmsnorm(x, norm_final)
```

```python
import functools
import math

import jax
import jax.numpy as jnp
from jax import lax
from jax.experimental import pallas as pl
from jax.experimental.pallas import tpu as pltpu

F32 = jnp.float32
BF16 = jnp.bfloat16

RMS_EPS = 1e-6
LANES = 128
VMEM_LIMIT = 60 * 1024 * 1024

SB_HEADS = 8
SB_HEAD_DIM = 128
SSD_HEAD_DIM = 64
SSD_GROUPS = 8
SSD_STATE = 128
SSD_CHUNK = 128
PEER_HEADS = 8
PEER_KEYS = 128
PEER_TOPK = 16
PEER_ROW_LEN = (16, 8, 5, 4, 3, 2, 2, 2)


def _params(*sem):
    return pltpu.CompilerParams(dimension_semantics=sem, vmem_limit_bytes=VMEM_LIMIT)


def _split3(v):
    hi = v.astype(BF16)
    r = v - hi.astype(F32)
    mid = r.astype(BF16)
    lo = (r - mid.astype(F32)).astype(BF16)
    return hi, mid, lo


def _dot(a, b):
    return jnp.dot(a, b, preferred_element_type=F32)


def _dot_nt(a, b):
    return lax.dot_general(a, b, (((1,), (1,)), ((), ())), preferred_element_type=F32)


def _norm_matmul_kernel(x_ref, nw_ref, w_ref, o_ref, xn_ref):
    @pl.when(pl.program_id(1) == 0)
    def _():
        x = x_ref[...]
        ms = jnp.mean(x * x, axis=-1, keepdims=True)
        xn_ref[...] = (x * lax.rsqrt(ms + RMS_EPS) * nw_ref[...]).astype(BF16)

    o_ref[...] = _dot(xn_ref[...], w_ref[...]).astype(o_ref.dtype)


def norm_matmul(x, nw, w, out_dtype, tm, tn):
    m, k = x.shape
    n = w.shape[1]
    return pl.pallas_call(
        _norm_matmul_kernel,
        grid=(m // tm, n // tn),
        in_specs=[pl.BlockSpec((tm, k), lambda i, j: (i, 0)),
                  pl.BlockSpec((1, k), lambda i, j: (0, 0)),
                  pl.BlockSpec((k, tn), lambda i, j: (0, j))],
        out_specs=pl.BlockSpec((tm, tn), lambda i, j: (i, j)),
        out_shape=jax.ShapeDtypeStruct((m, n), out_dtype),
        scratch_shapes=[pltpu.VMEM((tm, k), BF16)],
        compiler_params=_params("parallel", "arbitrary"),
        name="norm_matmul",
    )(x, nw.reshape(1, k), w)


def _matmul_resid_kernel(*refs, n_in):
    a_refs = refs[:n_in]
    w_refs = refs[n_in:2 * n_in]
    r_ref = refs[2 * n_in]
    o_ref = refs[2 * n_in + 1]
    acc = r_ref[...]
    for a_ref, w_ref in zip(a_refs, w_refs):
        acc = acc + _dot(a_ref[...], w_ref[...])
    o_ref[...] = acc


def matmul_resid(a_list, w_list, resid, tm, tn):
    m, n = resid.shape
    n_in = len(a_list)
    in_specs = ([pl.BlockSpec((tm, a.shape[1]), lambda i, j: (i, 0)) for a in a_list]
                + [pl.BlockSpec((w.shape[0], tn), lambda i, j: (0, j)) for w in w_list]
                + [pl.BlockSpec((tm, tn), lambda i, j: (i, j))])
    return pl.pallas_call(
        functools.partial(_matmul_resid_kernel, n_in=n_in),
        grid=(m // tm, n // tn),
        in_specs=in_specs,
        out_specs=pl.BlockSpec((tm, tn), lambda i, j: (i, j)),
        out_shape=jax.ShapeDtypeStruct((m, n), F32),
        compiler_params=_params("parallel", "arbitrary"),
        name="matmul_resid",
    )(*a_list, *w_list, resid)


def _causal_conv(u, w_ref):
    k_w = w_ref.shape[0]
    row = lax.broadcasted_iota(jnp.int32, u.shape, 0)
    out = u * w_ref[k_w - 1:k_w, :]
    for k in range(k_w - 1):
        shift = k_w - 1 - k
        shifted = jnp.where(row >= shift, pltpu.roll(u, shift, 0), 0.0)
        out = out + shifted * w_ref[k:k + 1, :]
    return out


def _even_conv_kernel(gb_ref, gc_ref, xv_ref, w_ref, o_ref):
    u = gc_ref[...] * xv_ref[...]
    o_ref[...] = (gb_ref[...] * _causal_conv(u, w_ref)).astype(o_ref.dtype)


def even_conv(proj, conv_w, batch, seq, cb):
    c = conv_w.shape[1]
    nb = c // cb
    return pl.pallas_call(
        _even_conv_kernel,
        grid=(batch, nb),
        in_specs=[pl.BlockSpec((seq, cb), lambda b, j: (b, j)),
                  pl.BlockSpec((seq, cb), lambda b, j: (b, nb + j)),
                  pl.BlockSpec((seq, cb), lambda b, j: (b, 2 * nb + j)),
                  pl.BlockSpec((conv_w.shape[0], cb), lambda b, j: (0, j))],
        out_specs=pl.BlockSpec((seq, cb), lambda b, j: (b, j)),
        out_shape=jax.ShapeDtypeStruct((batch * seq, c), BF16),
        compiler_params=_params("parallel", "parallel"),
        name="even_conv",
    )(proj, proj, proj, conv_w)


def _ssd_conv_kernel(u_ref, w_ref, b_ref, o_ref):
    y = _causal_conv(u_ref[...], w_ref) + b_ref[...]
    o_ref[...] = (y * jax.nn.sigmoid(y)).astype(o_ref.dtype)


def ssd_conv(zx, conv_w, conv_b, batch, seq, col0, cb):
    c = conv_w.shape[1]
    off = col0 // cb
    return pl.pallas_call(
        _ssd_conv_kernel,
        grid=(batch, c // cb),
        in_specs=[pl.BlockSpec((seq, cb), lambda b, j: (b, off + j)),
                  pl.BlockSpec((conv_w.shape[0], cb), lambda b, j: (0, j)),
                  pl.BlockSpec((1, cb), lambda b, j: (0, j))],
        out_specs=pl.BlockSpec((seq, cb), lambda b, j: (b, j)),
        out_shape=jax.ShapeDtypeStruct((batch * seq, c), F32),
        compiler_params=_params("parallel", "parallel"),
        name="ssd_conv",
    )(zx, conv_w, conv_b.reshape(1, c))


def _sb_attn_kernel(q_ref, k_ref, v_ref, o_ref, *, blk, scale):
    qi = pl.program_id(2)
    q = q_ref[...]
    jj = lax.broadcasted_iota(jnp.int32, (blk, 2 * blk), 0)
    ss = lax.broadcasted_iota(jnp.int32, (blk, 2 * blk), 1)
    cum_mat = jnp.where((ss >= blk) | (jj > ss), 1.0, 0.0).astype(BF16)
    row = lax.broadcasted_iota(jnp.int32, (blk, blk), 0)
    col = lax.broadcasted_iota(jnp.int32, (blk, blk), 1)
    causal = col < row

    def block(kb, carry, acc, diag):
        start = pl.multiple_of(kb * blk, blk)
        k = k_ref[pl.ds(start, blk), :]
        v = v_ref[pl.ds(start, blk), :]
        z = _dot_nt(q, k) * scale
        ls = -(jnp.maximum(z, 0.0) + jnp.log1p(jnp.exp(-jnp.abs(z))))
        if diag:
            ls = jnp.where(causal, ls, 0.0)
        hi, mid, lo = _split3(ls)
        cs = _dot(hi, cum_mat) + _dot(mid, cum_mat) + _dot(lo, cum_mat)
        a = jnp.exp(z + ls + cs[:, :blk] + carry)
        if diag:
            a = jnp.where(causal, a, 0.0)
        acc = acc + _dot(a.astype(BF16), v)
        return carry + cs[:, blk:], acc

    carry = jnp.zeros((blk, blk), F32)
    acc = jnp.zeros((blk, q.shape[1]), F32)
    carry, acc = block(qi, carry, acc, True)

    def body(it, state):
        return block(qi - 1 - it, state[0], state[1], False)

    carry, acc = lax.fori_loop(0, qi, body, (carry, acc))
    o_ref[...] = acc.astype(o_ref.dtype)


def sb_attention(qkv, batch, seq, blk):
    d = SB_HEAD_DIM
    nq = seq // blk
    return pl.pallas_call(
        functools.partial(_sb_attn_kernel, blk=blk, scale=d ** -0.5),
        grid=(batch, SB_HEADS, nq),
        in_specs=[pl.BlockSpec((blk, d), lambda b, h, i: (b * nq + i, h)),
                  pl.BlockSpec((seq, d), lambda b, h, i: (b, SB_HEADS + h)),
                  pl.BlockSpec((seq, d), lambda b, h, i: (b, 2 * SB_HEADS + h))],
        out_specs=pl.BlockSpec((blk, d), lambda b, h, i: (b * nq + i, h)),
        out_shape=jax.ShapeDtypeStruct((batch * seq, SB_HEADS * d), BF16),
        compiler_params=_params("parallel", "parallel", "arbitrary"),
        name="sb_attention",
    )(qkv, qkv, qkv)


def _softplus(x):
    return jnp.maximum(x, 0.0) + jnp.log1p(jnp.exp(-jnp.abs(x)))


def _ssd_scan_kernel(x_ref, b_ref, c_ref, z_ref, dtc_ref, dtr_ref, pc_ref, pr_ref, nw_ref,
                     o_ref, st_ref, y_ref, xw_ref):
    L = x_ref.shape[0]
    hpg = dtc_ref.shape[-1]
    P = SSD_HEAD_DIM

    @pl.when(pl.program_id(2) == 0)
    def _():
        st_ref[...] = jnp.zeros_like(st_ref)

    a_c = -jnp.exp(pc_ref[0, 1:2, :])
    a_r = -jnp.exp(pr_ref[0, :, 1:2])
    dt_c = _softplus(dtc_ref[0, 0] + pc_ref[0, 0:1, :])
    dt_r = _softplus(dtr_ref[0, 0] + pr_ref[0, :, 0:1])
    ti = lax.broadcasted_iota(jnp.int32, (L, L), 0)
    si = lax.broadcasted_iota(jnp.int32, (L, L), 1)
    tri = jnp.where(si <= ti, 1.0, 0.0).astype(BF16)
    tri_t = jnp.where(ti <= si, 1.0, 0.0).astype(BF16)
    h0, h1, h2 = _split3(dt_c * a_c)
    acum_c = _dot(tri, h0) + _dot(tri, h1) + _dot(tri, h2)
    g0, g1, g2 = _split3(dt_r * a_r)
    acum_r = _dot(g0, tri_t) + _dot(g1, tri_t) + _dot(g2, tri_t)

    x = x_ref[...]
    bq = b_ref[...].astype(BF16)
    cq = c_ref[...].astype(BF16)
    cb = _dot_nt(cq, bq)
    st = st_ref[...]
    y_inter = _dot(cq, st.astype(BF16))
    e_c = jnp.exp(acum_c)
    w_c = jnp.exp(acum_c[L - 1:L, :] - acum_c) * dt_c
    causal = si <= ti
    for h in range(hpg):
        sl = slice(h * P, (h + 1) * P)
        seg = acum_c[:, h:h + 1] - acum_r[h:h + 1, :]
        decay = jnp.exp(jnp.where(causal, seg, -jnp.inf))
        m = cb * decay * dt_r[h:h + 1, :]
        xh = x[:, sl]
        y_h = _dot(m.astype(BF16), xh.astype(BF16))
        y_ref[:, sl] = y_h + y_inter[:, sl] * e_c[:, h:h + 1] + xh * pc_ref[0, 2:3, h:h + 1]
        xw_ref[:, sl] = xh * w_c[:, h:h + 1]
    upd = _dot(b_ref[...].T.astype(BF16), xw_ref[...].astype(BF16))
    e_last = jnp.exp(acum_r[:, L - 1:L])
    for h in range(hpg):
        sl = slice(h * P, (h + 1) * P)
        st_ref[:, sl] = st[:, sl] * e_last[h:h + 1, :] + upd[:, sl]

    zz = z_ref[...]
    yg = y_ref[...] * (zz * jax.nn.sigmoid(zz))
    ms = jnp.mean(yg * yg, axis=-1, keepdims=True)
    o_ref[...] = (yg * lax.rsqrt(ms + RMS_EPS) * nw_ref[...]).astype(o_ref.dtype)


def ssd_scan(zx, xbc, dt_raw, dt_bias, a_log, d_skip, norm_w, batch, seq):
    L = SSD_CHUNK
    G = SSD_GROUPS
    N = SSD_STATE
    heads = dt_raw.shape[1]
    hpg = heads // G
    gw = hpg * SSD_HEAD_DIM
    d_inner = heads * SSD_HEAD_DIM
    nc = seq // L
    n_chunks = batch * nc
    dt4 = dt_raw.reshape(n_chunks, L, G, hpg)
    dt_col = dt4.transpose(0, 2, 1, 3)
    dt_row = dt4.transpose(0, 2, 3, 1)
    par = jnp.stack([dt_bias, a_log, d_skip], axis=0).astype(F32).reshape(3, G, hpg)
    par_col = par.transpose(1, 0, 2)
    par_row = par.transpose(1, 2, 0)
    return pl.pallas_call(
        _ssd_scan_kernel,
        grid=(batch, G, nc),
        in_specs=[pl.BlockSpec((L, gw), lambda b, g, c: (b * nc + c, g)),
                  pl.BlockSpec((L, N), lambda b, g, c: (b * nc + c, d_inner // N + g)),
                  pl.BlockSpec((L, N), lambda b, g, c: (b * nc + c, d_inner // N + G + g)),
                  pl.BlockSpec((L, gw), lambda b, g, c: (b * nc + c, g)),
                  pl.BlockSpec((1, 1, L, hpg), lambda b, g, c: (b * nc + c, g, 0, 0)),
                  pl.BlockSpec((1, 1, hpg, L), lambda b, g, c: (b * nc + c, g, 0, 0)),
                  pl.BlockSpec((1, 3, hpg), lambda b, g, c: (g, 0, 0)),
                  pl.BlockSpec((1, hpg, 3), lambda b, g, c: (g, 0, 0)),
                  pl.BlockSpec((1, gw), lambda b, g, c: (0, g))],
        out_specs=pl.BlockSpec((L, gw), lambda b, g, c: (b * nc + c, g)),
        out_shape=jax.ShapeDtypeStruct((batch * seq, d_inner), BF16),
        scratch_shapes=[pltpu.VMEM((N, gw), F32), pltpu.VMEM((L, gw), F32), pltpu.VMEM((L, gw), F32)],
        compiler_params=_params("parallel", "parallel", "arbitrary"),
        name="ssd_scan",
    )(xbc, xbc, xbc, zx, dt_col, dt_row, par_col, par_row, norm_w.reshape(1, d_inner))


def _top16(s):
    r = s.shape[0]
    rows = lax.broadcasted_iota(jnp.int32, s.shape, 0)
    krow = lax.broadcasted_iota(jnp.int32, (PEER_TOPK, s.shape[1]), 0)
    rank = jnp.full(s.shape, PEER_TOPK, jnp.int32)
    vals = jnp.zeros((PEER_TOPK, s.shape[1]), F32)
    for i in range(PEER_TOPK):
        m = jnp.max(s, axis=0, keepdims=True)
        idx = jnp.min(jnp.where(s == m, rows, r), axis=0, keepdims=True)
        hit = rows == idx
        rank = jnp.where(hit, i, rank)
        s = jnp.where(hit, -jnp.inf, s)
        vals = jnp.where(krow == i, m, vals)
    return vals, rank


def _peer_route_kernel(x_ref, nw_ref, wq_ref, keys_ref, xn_ref, r2_ref, e2_ref, n1_ref, f1_ref):
    x = x_ref[...]
    ms = jnp.mean(x * x, axis=-1, keepdims=True)
    xn = (x * lax.rsqrt(ms + RMS_EPS) * nw_ref[...]).astype(BF16)
    xn_ref[...] = xn
    q = _dot(xn, wq_ref[...]).astype(BF16)
    half = keys_ref.shape[-1]
    j8 = lax.broadcasted_iota(jnp.int32, (8, x.shape[0]), 0)
    for h in range(PEER_HEADS):
        s1 = _dot_nt(keys_ref[h, 0], q[:, (2 * h) * half:(2 * h + 1) * half])
        s2 = _dot_nt(keys_ref[h, 1], q[:, (2 * h + 1) * half:(2 * h + 2) * half])
        a, r1 = _top16(s1)
        b, r2 = _top16(s2)
        blocks = [a[0:1] + b]
        for i in range(1, 8):
            blk = a[i:i + 1] + b[0:8]
            if PEER_ROW_LEN[i] < 8:
                blk = jnp.where(j8 < PEER_ROW_LEN[i], blk, -jnp.inf)
            blocks.append(blk)
        blocks.append(a[8:16] + b[0:1])
        cand = jnp.concatenate(blocks, axis=0)
        _, rc = _top16(cand)
        sel = rc < PEER_TOPK
        cmax = a[0:1] + b[0:1]
        z = jnp.sum(jnp.where(sel, jnp.exp(cand - cmax), 0.0), axis=0, keepdims=True)
        self32 = jnp.where(sel, 1.0, 0.0)
        n1 = jnp.zeros(s1.shape, F32)
        off = 0
        for i in range(8):
            ln = 16 if i == 0 else 8
            cnt = jnp.sum(self32[off:off + ln], axis=0, keepdims=True)
            n1 = jnp.where(r1 == i, cnt, n1)
            off += ln
        for i in range(8, 16):
            n1 = jnp.where(r1 == i, self32[off + i - 8:off + i - 7], n1)
        r2_ref[h] = r2.astype(F32)
        n1_ref[h] = n1
        e2_ref[h] = jnp.where(r2 < PEER_TOPK, jnp.exp(s2 - b[0:1]), 0.0)
        f1_ref[h] = jnp.where(r1 < PEER_TOPK, jnp.exp(s1 - a[0:1]), 0.0) / z


def peer_route(x, nw, w_q, sub_keys, tm):
    t, d = x.shape
    nk = sub_keys.shape[2]
    fac = jax.ShapeDtypeStruct((PEER_HEADS, nk, t), F32)
    fspec = pl.BlockSpec((PEER_HEADS, nk, tm), lambda i: (0, 0, i))
    return pl.pallas_call(
        _peer_route_kernel,
        grid=(t // tm,),
        in_specs=[pl.BlockSpec((tm, d), lambda i: (i, 0)),
                  pl.BlockSpec((1, d), lambda i: (0, 0)),
                  pl.BlockSpec(w_q.shape, lambda i: (0, 0)),
                  pl.BlockSpec(sub_keys.shape, lambda i: (0, 0, 0, 0))],
        out_specs=[pl.BlockSpec((tm, d), lambda i: (i, 0)), fspec, fspec, fspec, fspec],
        out_shape=[jax.ShapeDtypeStruct((t, d), BF16), fac, fac, fac, fac],
        compiler_params=_params("parallel"),
        name="peer_route",
    )(x, nw.reshape(1, d), w_q, sub_keys)


def _gelu(h):
    return 0.5 * h * (1.0 + lax.erf(h * (1.0 / math.sqrt(2.0))))


def _peer_experts_kernel(xn_ref, u_ref, vt_ref, r2_ref, e2_ref, n1_ref, f1_ref, x_ref, o_ref, acc_ref,
                         *, n_keys):
    j = pl.program_id(1)

    @pl.when(j == 0)
    def _():
        acc_ref[...] = jnp.zeros_like(acc_ref)

    eb = u_ref.shape[0]
    ht = _dot_nt(u_ref[...], xn_ref[...])
    acts = []
    for cc in range(eb // n_keys):
        c = j * (eb // n_keys) + cc
        g = jnp.zeros((n_keys, ht.shape[1]), F32)
        for h in range(PEER_HEADS):
            n1 = n1_ref[h, pl.ds(c, 1), :]
            f1 = f1_ref[h, pl.ds(c, 1), :]
            g = g + jnp.where(r2_ref[h] < n1, e2_ref[h], 0.0) * f1
        acts.append((g * _gelu(ht[cc * n_keys:(cc + 1) * n_keys])).astype(BF16))
    act = jnp.concatenate(acts, axis=0)
    acc_ref[...] += _dot(vt_ref[...], act)

    @pl.when(j == pl.num_programs(1) - 1)
    def _():
        o_ref[...] = x_ref[...] + acc_ref[...].T


def peer_experts(x, xn, u, vt, r2, e2, n1, f1, tm, eb):
    t, d = x.shape
    n_exp = u.shape[0]
    n_keys = r2.shape[1]
    fspec = pl.BlockSpec((PEER_HEADS, n_keys, tm), lambda i, j: (0, 0, i))
    return pl.pallas_call(
        functools.partial(_peer_experts_kernel, n_keys=n_keys),
        grid=(t // tm, n_exp // eb),
        in_specs=[pl.BlockSpec((tm, d), lambda i, j: (i, 0)),
                  pl.BlockSpec((eb, d), lambda i, j: (j, 0)),
                  pl.BlockSpec((d, eb), lambda i, j: (0, j)),
                  fspec, fspec, fspec, fspec,
                  pl.BlockSpec((tm, d), lambda i, j: (i, 0))],
        out_specs=pl.BlockSpec((tm, d), lambda i, j: (i, 0)),
        out_shape=jax.ShapeDtypeStruct((t, d), F32),
        scratch_shapes=[pltpu.VMEM((d, tm), F32)],
        compiler_params=_params("parallel", "arbitrary"),
        name="peer_experts",
    )(xn, u, vt, r2, e2, n1, f1, x)


def peer_ffn(x, nw, w_q, sub_keys, u, v, route_tm, tm, eb):
    xn, r2, e2, n1, f1 = peer_route(x, nw, w_q.astype(BF16), sub_keys.astype(BF16), route_tm)
    return peer_experts(x, xn, u.astype(BF16), v.T.astype(BF16), r2, e2, n1, f1, tm, eb)


def _rmsnorm_kernel(x_ref, nw_ref, o_ref):
    x = x_ref[...]
    ms = jnp.mean(x * x, axis=-1, keepdims=True)
    o_ref[...] = x * lax.rsqrt(ms + RMS_EPS) * nw_ref[...]


def rmsnorm(x, nw, tm):
    t, d = x.shape
    return pl.pallas_call(
        _rmsnorm_kernel,
        grid=(t // tm,),
        in_specs=[pl.BlockSpec((tm, d), lambda i: (i, 0)), pl.BlockSpec((1, d), lambda i: (0, 0))],
        out_specs=pl.BlockSpec((tm, d), lambda i: (i, 0)),
        out_shape=jax.ShapeDtypeStruct((t, d), F32),
        compiler_params=_params("parallel"),
        name="rmsnorm",
    )(x, nw.reshape(1, d))


def _tile(n, pref):
    t = min(n, pref)
    while n % t:
        t -= LANES
    return t


def even_layer(x, nw, w_in, conv_w, w_out, batch, seq):
    t = x.shape[0]
    conv_ch = conv_w.shape[1]
    w_in = w_in.astype(BF16)
    tm = _tile(t, 1024)
    proj_a = norm_matmul(x, nw, w_in[:, :3 * conv_ch], F32, tm, _tile(3 * conv_ch, 512))
    qkv = norm_matmul(x, nw, w_in[:, 3 * conv_ch:], BF16, tm, _tile(w_in.shape[1] - 3 * conv_ch, 512))
    y_conv = even_conv(proj_a, conv_w, batch, seq, _tile(conv_ch, 256))
    y_sb = sb_attention(qkv, batch, seq, _tile(seq, 256))
    w_out = w_out.astype(BF16)
    return matmul_resid([y_conv, y_sb], [w_out[:conv_ch], w_out[conv_ch:]], x, tm, _tile(x.shape[1], 512))


def ssd_layer(x, nw, w_in, conv_w, conv_b, dt_bias, a_log, d_skip, norm_w, w_out, batch, seq):
    t = x.shape[0]
    heads = dt_bias.shape[0]
    d_inner = heads * SSD_HEAD_DIM
    conv_dim = conv_w.shape[1]
    w_in = w_in.astype(BF16)
    tm = _tile(t, 1024)
    zx = norm_matmul(x, nw, w_in[:, :d_inner + conv_dim], F32, tm, _tile(d_inner + conv_dim, 512))
    w_dt = jnp.pad(w_in[:, d_inner + conv_dim:], ((0, 0), (0, LANES - heads)))
    dt_raw = norm_matmul(x, nw, w_dt, F32, tm, LANES)[:, :heads]
    xbc = ssd_conv(zx, conv_w, conv_b, batch, seq, d_inner, _tile(conv_dim, 512))
    y = ssd_scan(zx, xbc, dt_raw, dt_bias, a_log, d_skip, norm_w, batch, seq)
    return matmul_resid([y], [w_out.astype(BF16)], x, tm, _tile(x.shape[1], 512))


def kernel(x, norm_mix, norm_ffn, norm_final, ev_w_in, ev_conv_w, ev_w_out, ssd_w_in, ssd_conv_w, ssd_conv_b, ssd_dt_bias, ssd_a_log, ssd_d, ssd_norm, ssd_w_out, peer_w_q, peer_sub_keys, peer_u, peer_v):
    batch, seq, d = x.shape
    t = batch * seq
    x = x.reshape(t, d)
    depth = norm_mix.shape[0]
    for i in range(depth):
        j = i // 2
        if i % 2 == 0:
            x = even_layer(x, norm_mix[i], ev_w_in[j], ev_conv_w[j], ev_w_out[j], batch, seq)
        else:
            x = ssd_layer(x, norm_mix[i], ssd_w_in[j], ssd_conv_w[j], ssd_conv_b[j], ssd_dt_bias[j],
                          ssd_a_log[j], ssd_d[j], ssd_norm[j], ssd_w_out[j], batch, seq)
        x = peer_ffn(x, norm_ffn[i], peer_w_q[i], peer_sub_keys[i], peer_u[i], peer_v[i],
                     _tile(t, 256), _tile(t, 512), 512)
    return rmsnorm(x, norm_final, _tile(t, 512)).reshape(batch, seq, d)
```

```python
import functools
import math

import jax
import jax.numpy as jnp
from jax import lax
from jax.experimental import pallas as pl
from jax.experimental.pallas import tpu as pltpu

F32 = jnp.float32
BF16 = jnp.bfloat16

RMS_EPS = 1e-6
LANES = 128
VMEM_LIMIT = 60 * 1024 * 1024

SB_HEADS = 8
SB_HEAD_DIM = 128
SSD_HEAD_DIM = 64
SSD_GROUPS = 8
SSD_STATE = 128
SSD_CHUNK = 128
PEER_HEADS = 8
PEER_KEYS = 128
PEER_TOPK = 16
PEER_ROW_LEN = (16, 8, 5, 4, 3, 2, 2, 2)


def _params(*sem):
    return pltpu.CompilerParams(dimension_semantics=sem, vmem_limit_bytes=VMEM_LIMIT)


def _split3(v):
    hi = v.astype(BF16)
    r = v - hi.astype(F32)
    mid = r.astype(BF16)
    lo = (r - mid.astype(F32)).astype(BF16)
    return hi, mid, lo


def _dot(a, b):
    return jnp.dot(a, b, preferred_element_type=F32)


def _dot_nt(a, b):
    return lax.dot_general(a, b, (((1,), (1,)), ((), ())), preferred_element_type=F32)


def _norm_matmul_kernel(x_ref, nw_ref, w_ref, o_ref, xn_ref):
    @pl.when(pl.program_id(1) == 0)
    def _():
        x = x_ref[...]
        ms = jnp.mean(x * x, axis=-1, keepdims=True)
        xn_ref[...] = (x * lax.rsqrt(ms + RMS_EPS) * nw_ref[...]).astype(BF16)

    o_ref[...] = _dot(xn_ref[...], w_ref[...]).astype(o_ref.dtype)


def norm_matmul(x, nw, w, out_dtype, tm, tn):
    m, k = x.shape
    n = w.shape[1]
    return pl.pallas_call(
        _norm_matmul_kernel,
        grid=(m // tm, n // tn),
        in_specs=[pl.BlockSpec((tm, k), lambda i, j: (i, 0)),
                  pl.BlockSpec((1, k), lambda i, j: (0, 0)),
                  pl.BlockSpec((k, tn), lambda i, j: (0, j))],
        out_specs=pl.BlockSpec((tm, tn), lambda i, j: (i, j)),
        out_shape=jax.ShapeDtypeStruct((m, n), out_dtype),
        scratch_shapes=[pltpu.VMEM((tm, k), BF16)],
        compiler_params=_params("parallel", "arbitrary"),
        name="norm_matmul",
    )(x, nw.reshape(1, k), w)


def _matmul_resid_kernel(*refs, n_in):
    a_refs = refs[:n_in]
    w_refs = refs[n_in:2 * n_in]
    r_ref = refs[2 * n_in]
    o_ref = refs[2 * n_in + 1]
    acc = r_ref[...]
    for a_ref, w_ref in zip(a_refs, w_refs):
        acc = acc + _dot(a_ref[...], w_ref[...])
    o_ref[...] = acc


def matmul_resid(a_list, w_list, resid, tm, tn):
    m, n = resid.shape
    n_in = len(a_list)
    in_specs = ([pl.BlockSpec((tm, a.shape[1]), lambda i, j: (i, 0)) for a in a_list]
                + [pl.BlockSpec((w.shape[0], tn), lambda i, j: (0, j)) for w in w_list]
                + [pl.BlockSpec((tm, tn), lambda i, j: (i, j))])
    return pl.pallas_call(
        functools.partial(_matmul_resid_kernel, n_in=n_in),
        grid=(m // tm, n // tn),
        in_specs=in_specs,
        out_specs=pl.BlockSpec((tm, tn), lambda i, j: (i, j)),
        out_shape=jax.ShapeDtypeStruct((m, n), F32),
        compiler_params=_params("parallel", "arbitrary"),
        name="matmul_resid",
    )(*a_list, *w_list, resid)


def _causal_conv(u, w_ref):
    k_w = w_ref.shape[0]
    row = lax.broadcasted_iota(jnp.int32, u.shape, 0)
    out = u * w_ref[k_w - 1:k_w, :]
    for k in range(k_w - 1):
        shift = k_w - 1 - k
        shifted = jnp.where(row >= shift, pltpu.roll(u, shift, 0), 0.0)
        out = out + shifted * w_ref[k:k + 1, :]
    return out


def _even_conv_kernel(gb_ref, gc_ref, xv_ref, w_ref, o_ref):
    u = gc_ref[...] * xv_ref[...]
    o_ref[...] = (gb_ref[...] * _causal_conv(u, w_ref)).astype(o_ref.dtype)


def even_conv(proj, conv_w, batch, seq, cb):
    c = conv_w.shape[1]
    nb = c // cb
    return pl.pallas_call(
        _even_conv_kernel,
        grid=(batch, nb),
        in_specs=[pl.BlockSpec((seq, cb), lambda b, j: (b, j)),
                  pl.BlockSpec((seq, cb), lambda b, j: (b, nb + j)),
                  pl.BlockSpec((seq, cb), lambda b, j: (b, 2 * nb + j)),
                  pl.BlockSpec((conv_w.shape[0], cb), lambda b, j: (0, j))],
        out_specs=pl.BlockSpec((seq, cb), lambda b, j: (b, j)),
        out_shape=jax.ShapeDtypeStruct((batch * seq, c), BF16),
        compiler_params=_params("parallel", "parallel"),
        name="even_conv",
    )(proj, proj, proj, conv_w)


def _ssd_conv_kernel(u_ref, w_ref, b_ref, o_ref):
    y = _causal_conv(u_ref[...], w_ref) + b_ref[...]
    o_ref[...] = (y * jax.nn.sigmoid(y)).astype(o_ref.dtype)


def ssd_conv(zx, conv_w, conv_b, batch, seq, col0, cb):
    c = conv_w.shape[1]
    off = col0 // cb
    return pl.pallas_call(
        _ssd_conv_kernel,
        grid=(batch, c // cb),
        in_specs=[pl.BlockSpec((seq, cb), lambda b, j: (b, off + j)),
                  pl.BlockSpec((conv_w.shape[0], cb), lambda b, j: (0, j)),
                  pl.BlockSpec((1, cb), lambda b, j: (0, j))],
        out_specs=pl.BlockSpec((seq, cb), lambda b, j: (b, j)),
        out_shape=jax.ShapeDtypeStruct((batch * seq, c), F32),
        compiler_params=_params("parallel", "parallel"),
        name="ssd_conv",
    )(zx, conv_w, conv_b.reshape(1, c))


def _sb_attn_kernel(q_ref, k_ref, v_ref, o_ref, *, blk, scale):
    qi = pl.program_id(2)
    q = q_ref[...]
    w = LANES
    nsub = blk // w
    jj = lax.broadcasted_iota(jnp.int32, (w, 2 * w), 0)
    ss = lax.broadcasted_iota(jnp.int32, (w, 2 * w), 1)
    cum_mat = jnp.where((ss >= w) | (jj > ss), 1.0, 0.0).astype(BF16)
    row = lax.broadcasted_iota(jnp.int32, (blk, blk), 0)
    col = lax.broadcasted_iota(jnp.int32, (blk, blk), 1)
    causal = col < row

    def block(kb, carry, acc, diag):
        start = pl.multiple_of(kb * blk, blk)
        k = k_ref[pl.ds(start, blk), :]
        v = v_ref[pl.ds(start, blk), :]
        z = _dot_nt(q, k) * scale
        ls = -(jnp.maximum(z, 0.0) + jnp.log(1.0 + jnp.exp(-jnp.abs(z))))
        if diag:
            ls = jnp.where(causal, ls, 0.0)
        hi = ls.astype(BF16)
        lo = (ls - hi.astype(F32)).astype(BF16)
        hi = jnp.concatenate([hi[:, i * w:(i + 1) * w] for i in range(nsub)], axis=0)
        lo = jnp.concatenate([lo[:, i * w:(i + 1) * w] for i in range(nsub)], axis=0)
        cs = _dot(hi, cum_mat) + _dot(lo, cum_mat)
        after = [None] * nsub
        for i in reversed(range(nsub)):
            after[i] = cs[i * blk:(i + 1) * blk, :w] + carry
            carry = carry + cs[i * blk:(i + 1) * blk, w:]
        a = jnp.exp(z + ls + jnp.concatenate(after, axis=1))
        if diag:
            a = jnp.where(causal, a, 0.0)
        acc = acc + _dot(a.astype(BF16), v)
        return carry, acc

    carry = jnp.zeros((blk, w), F32)
    acc = jnp.zeros((blk, q.shape[1]), F32)
    carry, acc = block(qi, carry, acc, True)
    carry, acc = lax.cond(qi % 2 == 1,
                          lambda c, a: block(qi - 1, c, a, False),
                          lambda c, a: (c, a), carry, acc)
    top = qi - qi % 2

    def body(it, state):
        c, a = block(top - 1 - 2 * it, state[0], state[1], False)
        return block(top - 2 - 2 * it, c, a, False)

    carry, acc = lax.fori_loop(0, qi // 2, body, (carry, acc))
    o_ref[...] = acc.astype(o_ref.dtype)


def sb_attention(qkv, batch, seq, blk):
    d = SB_HEAD_DIM
    nq = seq // blk
    return pl.pallas_call(
        functools.partial(_sb_attn_kernel, blk=blk, scale=d ** -0.5),
        grid=(batch, SB_HEADS, nq),
        in_specs=[pl.BlockSpec((blk, d), lambda b, h, i: (b * nq + i, h)),
                  pl.BlockSpec((seq, d), lambda b, h, i: (b, SB_HEADS + h)),
                  pl.BlockSpec((seq, d), lambda b, h, i: (b, 2 * SB_HEADS + h))],
        out_specs=pl.BlockSpec((blk, d), lambda b, h, i: (b * nq + i, h)),
        out_shape=jax.ShapeDtypeStruct((batch * seq, SB_HEADS * d), BF16),
        compiler_params=_params("parallel", "parallel", "arbitrary"),
        name="sb_attention",
    )(qkv, qkv, qkv)


def _softplus(x):
    return jnp.maximum(x, 0.0) + jnp.log1p(jnp.exp(-jnp.abs(x)))


def _onehot3(k, n, width):
    rr = lax.broadcasted_iota(jnp.int32, (3 * k, n), 0) % k
    cc = lax.broadcasted_iota(jnp.int32, (3 * k, n), 1) // width
    return jnp.where(rr == cc, 1.0, 0.0).astype(BF16)


def _expand(v, mat3):
    return _dot(jnp.concatenate(_split3(v), axis=1), mat3)


def _ssd_group(x, b, c, z, dt_c, dt_r, bias_c, alog_c, bias_r, alog_r, d_rep, nw, st):
    L, gw = x.shape
    hpg = dt_c.shape[-1]
    P = SSD_HEAD_DIM
    dt_c = _softplus(dt_c + bias_c)
    dt_r = _softplus(dt_r + bias_r)
    ti = lax.broadcasted_iota(jnp.int32, (L, L), 0)
    si = lax.broadcasted_iota(jnp.int32, (L, L), 1)
    tri = jnp.where(si <= ti, 1.0, 0.0).astype(BF16)
    tri_t3 = jnp.concatenate([jnp.where(ti <= si, 1.0, 0.0).astype(BF16)] * 3, axis=0)
    h0, h1, h2 = _split3(dt_c * -jnp.exp(alog_c))
    acum_c = _dot(tri, h0) + _dot(tri, h1) + _dot(tri, h2)
    acum_r = _expand(dt_r * -jnp.exp(alog_r), tri_t3)

    bq = b.astype(BF16)
    cq = c.astype(BF16)
    cb = _dot_nt(cq, bq)
    seg = _expand(acum_c, _onehot3(hpg, hpg * L, L)) - jnp.concatenate(
        [jnp.broadcast_to(acum_r[h:h + 1, :], (L, L)) for h in range(hpg)], axis=1)
    causal = jnp.concatenate([si <= ti] * hpg, axis=1)
    dt_s = jnp.concatenate([jnp.broadcast_to(dt_r[h:h + 1, :], (L, L)) for h in range(hpg)], axis=1)
    m = jnp.concatenate([cb] * hpg, axis=1) * jnp.exp(jnp.where(causal, seg, -jnp.inf)) * dt_s
    xb = x.astype(BF16)
    ch_head = lax.broadcasted_iota(jnp.int32, (L, gw), 1) // P
    x_bd = jnp.concatenate([jnp.where(ch_head == h, xb, jnp.zeros_like(xb)) for h in range(hpg)], axis=0)
    y_intra = _dot(m.astype(BF16), x_bd)

    ew = jnp.concatenate([jnp.exp(acum_c), jnp.exp(acum_c[L - 1:L, :] - acum_c) * dt_c], axis=1)
    ew = _expand(ew, _onehot3(2 * hpg, 2 * gw, P))
    e_ch = ew[:, :gw]
    w_ch = ew[:, gw:]
    y = y_intra + _dot(cq, st.astype(BF16)) * e_ch + x * d_rep
    upd = _dot(b.T.astype(BF16), (x * w_ch).astype(BF16))
    st = st * e_ch[L - 1:L, :] + upd

    yg = y * (z * jax.nn.sigmoid(z))
    ms = jnp.mean(yg * yg, axis=-1, keepdims=True)
    return yg * lax.rsqrt(ms + RMS_EPS) * nw, st


def _ssd_scan_kernel(x_ref, b_ref, c_ref, z_ref, dtc_ref, dtr_ref, pc_ref, pr_ref, drep_ref, nw_ref,
                     o_ref, st_ref):
    n_state = b_ref.shape[1] // st_ref.shape[0]
    gw = x_ref.shape[1] // st_ref.shape[0]

    @pl.when(pl.program_id(2) == 0)
    def _():
        st_ref[...] = jnp.zeros_like(st_ref)

    for g in range(st_ref.shape[0]):
        ch = slice(g * gw, (g + 1) * gw)
        ns = slice(g * n_state, (g + 1) * n_state)
        y, st = _ssd_group(x_ref[:, ch], b_ref[:, ns], c_ref[:, ns], z_ref[:, ch], dtc_ref[0, g], dtr_ref[0, g],
                           pc_ref[g, 0:1, :], pc_ref[g, 1:2, :], pr_ref[g, :, 0:1], pr_ref[g, :, 1:2],
                           drep_ref[:, ch], nw_ref[:, ch], st_ref[g])
        st_ref[g] = st
        o_ref[:, ch] = y.astype(o_ref.dtype)


def ssd_scan(zx, xbc, dt_raw, dt_bias, a_log, d_skip, norm_w, batch, seq, gps):
    L = SSD_CHUNK
    G = SSD_GROUPS
    N = SSD_STATE
    heads = dt_raw.shape[1]
    hpg = heads // G
    gw = hpg * SSD_HEAD_DIM
    d_inner = heads * SSD_HEAD_DIM
    nc = seq // L
    n_chunks = batch * nc
    dt4 = dt_raw.reshape(n_chunks, L, G, hpg)
    dt_col = dt4.transpose(0, 2, 1, 3)
    dt_row = dt4.transpose(0, 2, 3, 1)
    par = jnp.stack([dt_bias, a_log], axis=0).astype(F32).reshape(2, G, hpg)
    par_col = par.transpose(1, 0, 2)
    par_row = par.transpose(1, 2, 0)
    d_rep = jnp.repeat(d_skip.astype(F32), SSD_HEAD_DIM).reshape(1, d_inner)
    b_off = d_inner // (gps * N)
    return pl.pallas_call(
        _ssd_scan_kernel,
        grid=(batch, G // gps, nc),
        in_specs=[pl.BlockSpec((L, gps * gw), lambda b, g, c: (b * nc + c, g)),
                  pl.BlockSpec((L, gps * N), lambda b, g, c: (b * nc + c, b_off + g)),
                  pl.BlockSpec((L, gps * N), lambda b, g, c: (b * nc + c, b_off + G // gps + g)),
                  pl.BlockSpec((L, gps * gw), lambda b, g, c: (b * nc + c, g)),
                  pl.BlockSpec((1, gps, L, hpg), lambda b, g, c: (b * nc + c, g, 0, 0)),
                  pl.BlockSpec((1, gps, hpg, L), lambda b, g, c: (b * nc + c, g, 0, 0)),
                  pl.BlockSpec((gps, 2, hpg), lambda b, g, c: (g, 0, 0)),
                  pl.BlockSpec((gps, hpg, 2), lambda b, g, c: (g, 0, 0)),
                  pl.BlockSpec((1, gps * gw), lambda b, g, c: (0, g)),
                  pl.BlockSpec((1, gps * gw), lambda b, g, c: (0, g))],
        out_specs=pl.BlockSpec((L, gps * gw), lambda b, g, c: (b * nc + c, g)),
        out_shape=jax.ShapeDtypeStruct((batch * seq, d_inner), BF16),
        scratch_shapes=[pltpu.VMEM((gps, N, gw), F32)],
        compiler_params=_params("parallel", "parallel", "arbitrary"),
        name="ssd_scan",
    )(xbc, xbc, xbc, zx, dt_col, dt_row, par_col, par_row, d_rep, norm_w.reshape(1, d_inner))


def _top16(s):
    r = s.shape[0]
    rows = lax.broadcasted_iota(jnp.int32, s.shape, 0).astype(F32)
    krow = lax.broadcasted_iota(jnp.int32, (PEER_TOPK, s.shape[1]), 0)
    rank = jnp.full(s.shape, PEER_TOPK, jnp.int32)
    vals = jnp.zeros((PEER_TOPK, s.shape[1]), F32)
    for i in range(PEER_TOPK):
        m = jnp.max(s, axis=0, keepdims=True)
        idx = jnp.min(jnp.where(s == m, rows, float(r)), axis=0, keepdims=True)
        hit = rows == idx
        rank = jnp.where(hit, i, rank)
        s = jnp.where(hit, -jnp.inf, s)
        vals = jnp.where(krow == i, m, vals)
    return vals, rank


def _peer_route_kernel(x_ref, nw_ref, wq_ref, keys_ref, xn_ref, r2_ref, e2_ref, n1_ref, f1_ref):
    x = x_ref[...]
    ms = jnp.mean(x * x, axis=-1, keepdims=True)
    xn = (x * lax.rsqrt(ms + RMS_EPS) * nw_ref[...]).astype(BF16)
    xn_ref[...] = xn
    q = _dot(xn, wq_ref[...]).astype(BF16)
    half = keys_ref.shape[-1]
    j8 = lax.broadcasted_iota(jnp.int32, (8, x.shape[0]), 0)
    for h in range(PEER_HEADS):
        s1 = _dot_nt(keys_ref[h, 0], q[:, (2 * h) * half:(2 * h + 1) * half])
        s2 = _dot_nt(keys_ref[h, 1], q[:, (2 * h + 1) * half:(2 * h + 2) * half])
        a, r1 = _top16(s1)
        b, r2 = _top16(s2)
        blocks = [a[0:1] + b]
        for i in range(1, 8):
            blk = a[i:i + 1] + b[0:8]
            if PEER_ROW_LEN[i] < 8:
                blk = jnp.where(j8 < PEER_ROW_LEN[i], blk, -jnp.inf)
            blocks.append(blk)
        blocks.append(a[8:16] + b[0:1])
        cand = jnp.concatenate(blocks, axis=0)
        _, rc = _top16(cand)
        sel = rc < PEER_TOPK
        cmax = a[0:1] + b[0:1]
        z = jnp.sum(jnp.where(sel, jnp.exp(cand - cmax), 0.0), axis=0, keepdims=True)
        self32 = jnp.where(sel, 1.0, 0.0)
        n1 = jnp.zeros(s1.shape, F32)
        off = 0
        for i in range(8):
            ln = 16 if i == 0 else 8
            cnt = jnp.sum(self32[off:off + ln], axis=0, keepdims=True)
            n1 = jnp.where(r1 == i, cnt, n1)
            off += ln
        for i in range(8, 16):
            n1 = jnp.where(r1 == i, self32[off + i - 8:off + i - 7], n1)
        r2_ref[h] = r2.astype(F32).astype(r2_ref.dtype)
        n1_ref[h] = n1
        e2_ref[h] = jnp.where(r2 < PEER_TOPK, jnp.exp(s2 - b[0:1]), 0.0).astype(e2_ref.dtype)
        f1_ref[h] = jnp.where(r1 < PEER_TOPK, jnp.exp(s1 - a[0:1]), 0.0) / z


def peer_route(x, nw, w_q, sub_keys, tm):
    t, d = x.shape
    nk = sub_keys.shape[2]
    fac = jax.ShapeDtypeStruct((PEER_HEADS, nk, t), F32)
    fac16 = jax.ShapeDtypeStruct((PEER_HEADS, nk, t), BF16)
    fspec = pl.BlockSpec((PEER_HEADS, nk, tm), lambda i: (0, 0, i))
    return pl.pallas_call(
        _peer_route_kernel,
        grid=(t // tm,),
        in_specs=[pl.BlockSpec((tm, d), lambda i: (i, 0)),
                  pl.BlockSpec((1, d), lambda i: (0, 0)),
                  pl.BlockSpec(w_q.shape, lambda i: (0, 0)),
                  pl.BlockSpec(sub_keys.shape, lambda i: (0, 0, 0, 0))],
        out_specs=[pl.BlockSpec((tm, d), lambda i: (i, 0)), fspec, fspec, fspec, fspec],
        out_shape=[jax.ShapeDtypeStruct((t, d), BF16), fac16, fac16, fac, fac],
        compiler_params=_params("parallel"),
        name="peer_route",
    )(x, nw.reshape(1, d), w_q, sub_keys)


def _gelu(h):
    return 0.5 * h * (1.0 + lax.erf(h * (1.0 / math.sqrt(2.0))))


def _peer_experts_kernel(xn_ref, u_ref, vt_ref, r2_ref, e2_ref, n1_ref, f1_ref, x_ref, o_ref, acc_ref,
                         *, n_keys, sub):
    j = pl.program_id(1)

    @pl.when(j == 0)
    def _():
        acc_ref[...] = jnp.zeros_like(acc_ref)

    eb = u_ref.shape[0]
    m = xn_ref.shape[0]
    xn = xn_ref[...]
    rows16 = 16
    for sb in range(eb // sub):
        ht = _dot_nt(u_ref[sb * sub:(sb + 1) * sub, :], xn)
        acts = []
        for cc in range(sub // n_keys):
            c = (j * eb + sb * sub) // n_keys + cc
            g = jnp.zeros((n_keys, m), BF16)
            for h in range(PEER_HEADS):
                n1 = jnp.broadcast_to(n1_ref[h, pl.ds(c, 1), :], (rows16, m)).astype(BF16)
                f1 = jnp.broadcast_to(f1_ref[h, pl.ds(c, 1), :], (rows16, m)).astype(BF16)
                n1 = pltpu.repeat(n1, n_keys // rows16, axis=0)
                f1 = pltpu.repeat(f1, n_keys // rows16, axis=0)
                g = g + jnp.where(r2_ref[h] < n1, e2_ref[h], jnp.zeros_like(g)) * f1
            acts.append(g * _gelu(ht[cc * n_keys:(cc + 1) * n_keys]).astype(BF16))
        act = jnp.concatenate(acts, axis=0)
        acc_ref[...] = _dot(vt_ref[:, sb * sub:(sb + 1) * sub], act) + acc_ref[...]

    @pl.when(j == pl.num_programs(1) - 1)
    def _():
        o_ref[...] = x_ref[...] + acc_ref[...].T


def peer_experts(x, xn, u, vt, r2, e2, n1, f1, tm, eb, sub):
    t, d = x.shape
    n_exp = u.shape[0]
    n_keys = r2.shape[1]
    fspec = pl.BlockSpec((PEER_HEADS, n_keys, tm), lambda i, j: (0, 0, i))
    return pl.pallas_call(
        functools.partial(_peer_experts_kernel, n_keys=n_keys, sub=sub),
        grid=(t // tm, n_exp // eb),
        in_specs=[pl.BlockSpec((tm, d), lambda i, j: (i, 0)),
                  pl.BlockSpec((eb, d), lambda i, j: (j, 0)),
                  pl.BlockSpec((d, eb), lambda i, j: (0, j)),
                  fspec, fspec, fspec, fspec,
                  pl.BlockSpec((tm, d), lambda i, j: (i, 0))],
        out_specs=pl.BlockSpec((tm, d), lambda i, j: (i, 0)),
        out_shape=jax.ShapeDtypeStruct((t, d), F32),
        scratch_shapes=[pltpu.VMEM((d, tm), F32)],
        compiler_params=_params("parallel", "arbitrary"),
        name="peer_experts",
    )(xn, u, vt, r2, e2, n1, f1, x)


def peer_ffn(x, nw, w_q, sub_keys, u, v, route_tm, tm, eb, sub):
    xn, r2, e2, n1, f1 = peer_route(x, nw, w_q.astype(BF16), sub_keys.astype(BF16), route_tm)
    return peer_experts(x, xn, u.astype(BF16), v.T.astype(BF16), r2, e2, n1, f1, tm, eb, sub)


def _rmsnorm_kernel(x_ref, nw_ref, o_ref):
    x = x_ref[...]
    ms = jnp.mean(x * x, axis=-1, keepdims=True)
    o_ref[...] = x * lax.rsqrt(ms + RMS_EPS) * nw_ref[...]


def rmsnorm(x, nw, tm):
    t, d = x.shape
    return pl.pallas_call(
        _rmsnorm_kernel,
        grid=(t // tm,),
        in_specs=[pl.BlockSpec((tm, d), lambda i: (i, 0)), pl.BlockSpec((1, d), lambda i: (0, 0))],
        out_specs=pl.BlockSpec((tm, d), lambda i: (i, 0)),
        out_shape=jax.ShapeDtypeStruct((t, d), F32),
        compiler_params=_params("parallel"),
        name="rmsnorm",
    )(x, nw.reshape(1, d))


def _tile(n, pref):
    t = min(n, pref)
    while n % t:
        t -= LANES
    return t


def even_layer(x, nw, w_in, conv_w, w_out, batch, seq):
    t = x.shape[0]
    conv_ch = conv_w.shape[1]
    w_in = w_in.astype(BF16)
    tm = _tile(t, 1024)
    proj_a = norm_matmul(x, nw, w_in[:, :3 * conv_ch], F32, tm, _tile(3 * conv_ch, 512))
    qkv = norm_matmul(x, nw, w_in[:, 3 * conv_ch:], BF16, tm, _tile(w_in.shape[1] - 3 * conv_ch, 512))
    y_conv = even_conv(proj_a, conv_w, batch, seq, _tile(conv_ch, 256))
    y_sb = sb_attention(qkv, batch, seq, _tile(seq, 256))
    w_out = w_out.astype(BF16)
    return matmul_resid([y_conv, y_sb], [w_out[:conv_ch], w_out[conv_ch:]], x, tm, _tile(x.shape[1], 512))


def ssd_layer(x, nw, w_in, conv_w, conv_b, dt_bias, a_log, d_skip, norm_w, w_out, batch, seq):
    t = x.shape[0]
    heads = dt_bias.shape[0]
    d_inner = heads * SSD_HEAD_DIM
    conv_dim = conv_w.shape[1]
    w_in = w_in.astype(BF16)
    tm = _tile(t, 1024)
    zx = norm_matmul(x, nw, w_in[:, :d_inner + conv_dim], F32, tm, _tile(d_inner + conv_dim, 512))
    w_dt = jnp.pad(w_in[:, d_inner + conv_dim:], ((0, 0), (0, LANES - heads)))
    dt_raw = norm_matmul(x, nw, w_dt, F32, tm, LANES)[:, :heads]
    xbc = ssd_conv(zx, conv_w, conv_b, batch, seq, d_inner, _tile(conv_dim, 512))
    y = ssd_scan(zx, xbc, dt_raw, dt_bias, a_log, d_skip, norm_w, batch, seq, 2)
    return matmul_resid([y], [w_out.astype(BF16)], x, tm, _tile(x.shape[1], 512))


def kernel(x, norm_mix, norm_ffn, norm_final, ev_w_in, ev_conv_w, ev_w_out, ssd_w_in, ssd_conv_w, ssd_conv_b, ssd_dt_bias, ssd_a_log, ssd_d, ssd_norm, ssd_w_out, peer_w_q, peer_sub_keys, peer_u, peer_v):
    batch, seq, d = x.shape
    t = batch * seq
    x = x.reshape(t, d)
    depth = norm_mix.shape[0]
    for i in range(depth):
        j = i // 2
        if i % 2 == 0:
            x = even_layer(x, norm_mix[i], ev_w_in[j], ev_conv_w[j], ev_w_out[j], batch, seq)
        else:
            x = ssd_layer(x, norm_mix[i], ssd_w_in[j], ssd_conv_w[j], ssd_conv_b[j], ssd_dt_bias[j],
                          ssd_a_log[j], ssd_d[j], ssd_norm[j], ssd_w_out[j], batch, seq)
        x = peer_ffn(x, norm_ffn[i], peer_w_q[i], peer_sub_keys[i], peer_u[i], peer_v[i],
                     _tile(t, 128), _tile(t, 512), 1024, 512)
    return rmsnorm(x, norm_final, _tile(t, 512)).reshape(batch, seq, d)
```

```python
import functools
import math

import jax
import jax.numpy as jnp
from jax import lax
from jax.experimental import pallas as pl
from jax.experimental.pallas import tpu as pltpu

F32 = jnp.float32
BF16 = jnp.bfloat16

RMS_EPS = 1e-6
LANES = 128
VMEM_LIMIT = 60 * 1024 * 1024

SB_HEADS = 8
SB_HEAD_DIM = 128
SSD_HEAD_DIM = 64
SSD_GROUPS = 8
SSD_STATE = 128
SSD_CHUNK = 128
PEER_HEADS = 8
PEER_KEYS = 128
PEER_TOPK = 16
PEER_ROW_LEN = (16, 8, 5, 4, 3, 2, 2, 2)


def _params(*sem):
    return pltpu.CompilerParams(dimension_semantics=sem, vmem_limit_bytes=VMEM_LIMIT)


def _split3(v):
    hi = v.astype(BF16)
    r = v - hi.astype(F32)
    mid = r.astype(BF16)
    lo = (r - mid.astype(F32)).astype(BF16)
    return hi, mid, lo


def _dot(a, b):
    return jnp.dot(a, b, preferred_element_type=F32)


def _dot_nt(a, b):
    return lax.dot_general(a, b, (((1,), (1,)), ((), ())), preferred_element_type=F32)


def _norm_matmul_kernel(x_ref, nw_ref, w_ref, o_ref, xn_ref):
    @pl.when(pl.program_id(1) == 0)
    def _():
        x = x_ref[...]
        ms = jnp.mean(x * x, axis=-1, keepdims=True)
        xn_ref[...] = (x * lax.rsqrt(ms + RMS_EPS) * nw_ref[...]).astype(BF16)

    o_ref[...] = _dot(xn_ref[...], w_ref[...]).astype(o_ref.dtype)


def norm_matmul(x, nw, w, out_dtype, tm, tn):
    m, k = x.shape
    n = w.shape[1]
    return pl.pallas_call(
        _norm_matmul_kernel,
        grid=(m // tm, n // tn),
        in_specs=[pl.BlockSpec((tm, k), lambda i, j: (i, 0)),
                  pl.BlockSpec((1, k), lambda i, j: (0, 0)),
                  pl.BlockSpec((k, tn), lambda i, j: (0, j))],
        out_specs=pl.BlockSpec((tm, tn), lambda i, j: (i, j)),
        out_shape=jax.ShapeDtypeStruct((m, n), out_dtype),
        scratch_shapes=[pltpu.VMEM((tm, k), BF16)],
        compiler_params=_params("parallel", "arbitrary"),
        name="norm_matmul",
    )(x, nw.reshape(1, k), w)


def _matmul_resid_kernel(*refs, n_in):
    a_refs = refs[:n_in]
    w_refs = refs[n_in:2 * n_in]
    r_ref = refs[2 * n_in]
    o_ref = refs[2 * n_in + 1]
    acc = r_ref[...]
    for a_ref, w_ref in zip(a_refs, w_refs):
        acc = acc + _dot(a_ref[...], w_ref[...])
    o_ref[...] = acc


def matmul_resid(a_list, w_list, resid, tm, tn):
    m, n = resid.shape
    n_in = len(a_list)
    in_specs = ([pl.BlockSpec((tm, a.shape[1]), lambda i, j: (i, 0)) for a in a_list]
                + [pl.BlockSpec((w.shape[0], tn), lambda i, j: (0, j)) for w in w_list]
                + [pl.BlockSpec((tm, tn), lambda i, j: (i, j))])
    return pl.pallas_call(
        functools.partial(_matmul_resid_kernel, n_in=n_in),
        grid=(m // tm, n // tn),
        in_specs=in_specs,
        out_specs=pl.BlockSpec((tm, tn), lambda i, j: (i, j)),
        out_shape=jax.ShapeDtypeStruct((m, n), F32),
        compiler_params=_params("parallel", "arbitrary"),
        name="matmul_resid",
    )(*a_list, *w_list, resid)


def _causal_conv(u, w_ref):
    k_w = w_ref.shape[0]
    row = lax.broadcasted_iota(jnp.int32, u.shape, 0)
    out = u * w_ref[k_w - 1:k_w, :]
    for k in range(k_w - 1):
        shift = k_w - 1 - k
        shifted = jnp.where(row >= shift, pltpu.roll(u, shift, 0), 0.0)
        out = out + shifted * w_ref[k:k + 1, :]
    return out


def _even_conv_kernel(gb_ref, gc_ref, xv_ref, w_ref, o_ref):
    u = gc_ref[...] * xv_ref[...]
    o_ref[...] = (gb_ref[...] * _causal_conv(u, w_ref)).astype(o_ref.dtype)


def even_conv(proj, conv_w, batch, seq, cb):
    c = conv_w.shape[1]
    nb = c // cb
    return pl.pallas_call(
        _even_conv_kernel,
        grid=(batch, nb),
        in_specs=[pl.BlockSpec((seq, cb), lambda b, j: (b, j)),
                  pl.BlockSpec((seq, cb), lambda b, j: (b, nb + j)),
                  pl.BlockSpec((seq, cb), lambda b, j: (b, 2 * nb + j)),
                  pl.BlockSpec((conv_w.shape[0], cb), lambda b, j: (0, j))],
        out_specs=pl.BlockSpec((seq, cb), lambda b, j: (b, j)),
        out_shape=jax.ShapeDtypeStruct((batch * seq, c), BF16),
        compiler_params=_params("parallel", "parallel"),
        name="even_conv",
    )(proj, proj, proj, conv_w)


def _ssd_conv_kernel(u_ref, w_ref, b_ref, o_ref):
    y = _causal_conv(u_ref[...], w_ref) + b_ref[...]
    o_ref[...] = (y * jax.nn.sigmoid(y)).astype(o_ref.dtype)


def ssd_conv(zx, conv_w, conv_b, batch, seq, col0, cb):
    c = conv_w.shape[1]
    off = col0 // cb
    return pl.pallas_call(
        _ssd_conv_kernel,
        grid=(batch, c // cb),
        in_specs=[pl.BlockSpec((seq, cb), lambda b, j: (b, off + j)),
                  pl.BlockSpec((conv_w.shape[0], cb), lambda b, j: (0, j)),
                  pl.BlockSpec((1, cb), lambda b, j: (0, j))],
        out_specs=pl.BlockSpec((seq, cb), lambda b, j: (b, j)),
        out_shape=jax.ShapeDtypeStruct((batch * seq, c), F32),
        compiler_params=_params("parallel", "parallel"),
        name="ssd_conv",
    )(zx, conv_w, conv_b.reshape(1, c))


def _sb_attn_kernel(q_ref, k_ref, v_ref, o_ref, z_ref, a_ref, *, blk, scale):
    qi = pl.program_id(2)
    nq = pl.num_programs(2)
    q = q_ref[...]
    w = LANES
    nsub = blk // w
    jj = lax.broadcasted_iota(jnp.int32, (w, 2 * w), 0)
    ss = lax.broadcasted_iota(jnp.int32, (w, 2 * w), 1)
    cum_mat = jnp.where((ss >= w) | (jj > ss), 1.0, 0.0).astype(BF16)
    row = lax.broadcasted_iota(jnp.int32, (blk, blk), 0)
    col = lax.broadcasted_iota(jnp.int32, (blk, blk), 1)

    def rows_of(kb):
        return pl.ds(pl.multiple_of(jnp.clip(kb, 0, nq - 1) * blk, blk), blk)

    def scores(kb, slot):
        z_ref[slot] = _dot_nt(q, k_ref[rows_of(kb), :]) * scale

    def weights(kb, slot, carry):
        z = z_ref[slot]
        limit = jnp.where(kb == qi, row, jnp.where(kb < 0, 0, blk))
        keep = col < limit
        ls = -(jnp.maximum(z, 0.0) + jnp.log(1.0 + jnp.exp(-jnp.abs(z))))
        ls = jnp.where(keep, ls, 0.0)
        hi = ls.astype(BF16)
        lo = (ls - hi.astype(F32)).astype(BF16)
        hi = jnp.concatenate([hi[:, i * w:(i + 1) * w] for i in range(nsub)], axis=0)
        lo = jnp.concatenate([lo[:, i * w:(i + 1) * w] for i in range(nsub)], axis=0)
        cs = _dot(hi, cum_mat) + _dot(lo, cum_mat)
        after = [None] * nsub
        for i in reversed(range(nsub)):
            after[i] = cs[i * blk:(i + 1) * blk, :w] + carry
            carry = carry + cs[i * blk:(i + 1) * blk, w:]
        a = jnp.exp(z + ls + jnp.concatenate(after, axis=1))
        a_ref[slot] = jnp.where(keep, a, 0.0).astype(BF16)
        return carry

    def values(kb, slot, acc):
        return acc + _dot(a_ref[slot], v_ref[rows_of(kb), :])

    a_ref[...] = jnp.zeros_like(a_ref)
    scores(qi, 0)
    scores(qi - 1, 1)

    def body(it, state):
        carry, acc = state
        ka = qi - 2 * it
        acc = values(ka + 2, 0, acc)
        acc = values(ka + 1, 1, acc)
        carry = weights(ka, 0, carry)
        carry = weights(ka - 1, 1, carry)
        scores(ka - 2, 0)
        scores(ka - 3, 1)
        return carry, acc

    n_iter = (qi + 2) // 2
    carry = jnp.zeros((blk, w), F32)
    acc = jnp.zeros((blk, q.shape[1]), F32)
    carry, acc = lax.fori_loop(0, n_iter, body, (carry, acc))
    last = qi - 2 * (n_iter - 1)
    acc = values(last, 0, acc)
    acc = values(last - 1, 1, acc)
    o_ref[...] = acc.astype(o_ref.dtype)


def sb_attention(qkv, batch, seq, blk):
    d = SB_HEAD_DIM
    nq = seq // blk
    return pl.pallas_call(
        functools.partial(_sb_attn_kernel, blk=blk, scale=d ** -0.5),
        grid=(batch, SB_HEADS, nq),
        in_specs=[pl.BlockSpec((blk, d), lambda b, h, i: (b * nq + i, h)),
                  pl.BlockSpec((seq, d), lambda b, h, i: (b, SB_HEADS + h)),
                  pl.BlockSpec((seq, d), lambda b, h, i: (b, 2 * SB_HEADS + h))],
        out_specs=pl.BlockSpec((blk, d), lambda b, h, i: (b * nq + i, h)),
        out_shape=jax.ShapeDtypeStruct((batch * seq, SB_HEADS * d), BF16),
        scratch_shapes=[pltpu.VMEM((2, blk, blk), F32), pltpu.VMEM((2, blk, blk), BF16)],
        compiler_params=_params("parallel", "parallel", "arbitrary"),
        name="sb_attention",
    )(qkv, qkv, qkv)


def _softplus(x):
    return jnp.maximum(x, 0.0) + jnp.log1p(jnp.exp(-jnp.abs(x)))


def _onehot3(k, n, width):
    rr = lax.broadcasted_iota(jnp.int32, (3 * k, n), 0) % k
    cc = lax.broadcasted_iota(jnp.int32, (3 * k, n), 1) // width
    return jnp.where(rr == cc, 1.0, 0.0).astype(BF16)


def _expand(v, mat3):
    return _dot(jnp.concatenate(_split3(v), axis=1), mat3)


def _ssd_group(x, b, c, z, dt_c, dt_r, bias_c, alog_c, bias_r, alog_r, d_rep, nw, st):
    L, gw = x.shape
    hpg = dt_c.shape[-1]
    P = SSD_HEAD_DIM
    dt_c = _softplus(dt_c + bias_c)
    dt_r = _softplus(dt_r + bias_r)
    ti = lax.broadcasted_iota(jnp.int32, (L, L), 0)
    si = lax.broadcasted_iota(jnp.int32, (L, L), 1)
    tri = jnp.where(si <= ti, 1.0, 0.0).astype(BF16)
    tri_t3 = jnp.concatenate([jnp.where(ti <= si, 1.0, 0.0).astype(BF16)] * 3, axis=0)
    h0, h1, h2 = _split3(dt_c * -jnp.exp(alog_c))
    acum_c = _dot(tri, h0) + _dot(tri, h1) + _dot(tri, h2)
    acum_r = _expand(dt_r * -jnp.exp(alog_r), tri_t3)

    bq = b.astype(BF16)
    cq = c.astype(BF16)
    cb = _dot_nt(cq, bq)
    seg = _expand(acum_c, _onehot3(hpg, hpg * L, L)) - jnp.concatenate(
        [jnp.broadcast_to(acum_r[h:h + 1, :], (L, L)) for h in range(hpg)], axis=1)
    causal = jnp.concatenate([si <= ti] * hpg, axis=1)
    dt_s = jnp.concatenate([jnp.broadcast_to(dt_r[h:h + 1, :], (L, L)) for h in range(hpg)], axis=1)
    m = jnp.concatenate([cb] * hpg, axis=1) * jnp.exp(jnp.where(causal, seg, -jnp.inf)) * dt_s
    xb = x.astype(BF16)
    ch_head = lax.broadcasted_iota(jnp.int32, (L, gw), 1) // P
    x_bd = jnp.concatenate([jnp.where(ch_head == h, xb, jnp.zeros_like(xb)) for h in range(hpg)], axis=0)
    y_intra = _dot(m.astype(BF16), x_bd)

    ew = jnp.concatenate([jnp.exp(acum_c), jnp.exp(acum_c[L - 1:L, :] - acum_c) * dt_c], axis=1)
    ew = _expand(ew, _onehot3(2 * hpg, 2 * gw, P))
    e_ch = ew[:, :gw]
    w_ch = ew[:, gw:]
    y = y_intra + _dot(cq, st.astype(BF16)) * e_ch + x * d_rep
    upd = _dot(b.T.astype(BF16), (x * w_ch).astype(BF16))
    st = st * e_ch[L - 1:L, :] + upd

    yg = y * (z * jax.nn.sigmoid(z))
    ms = jnp.mean(yg * yg, axis=-1, keepdims=True)
    return yg * lax.rsqrt(ms + RMS_EPS) * nw, st


def _ssd_scan_kernel(x_ref, b_ref, c_ref, z_ref, dtc_ref, dtr_ref, pc_ref, pr_ref, drep_ref, nw_ref,
                     o_ref, st_ref):
    n_state = b_ref.shape[1] // st_ref.shape[0]
    gw = x_ref.shape[1] // st_ref.shape[0]

    @pl.when(pl.program_id(2) == 0)
    def _():
        st_ref[...] = jnp.zeros_like(st_ref)

    for g in range(st_ref.shape[0]):
        ch = slice(g * gw, (g + 1) * gw)
        ns = slice(g * n_state, (g + 1) * n_state)
        y, st = _ssd_group(x_ref[:, ch], b_ref[:, ns], c_ref[:, ns], z_ref[:, ch], dtc_ref[0, g], dtr_ref[0, g],
                           pc_ref[g, 0:1, :], pc_ref[g, 1:2, :], pr_ref[g, :, 0:1], pr_ref[g, :, 1:2],
                           drep_ref[:, ch], nw_ref[:, ch], st_ref[g])
        st_ref[g] = st
        o_ref[:, ch] = y.astype(o_ref.dtype)


def ssd_scan(zx, xbc, dt_raw, dt_bias, a_log, d_skip, norm_w, batch, seq, gps):
    L = SSD_CHUNK
    G = SSD_GROUPS
    N = SSD_STATE
    heads = dt_raw.shape[1]
    hpg = heads // G
    gw = hpg * SSD_HEAD_DIM
    d_inner = heads * SSD_HEAD_DIM
    nc = seq // L
    n_chunks = batch * nc
    dt4 = dt_raw.reshape(n_chunks, L, G, hpg)
    dt_col = dt4.transpose(0, 2, 1, 3)
    dt_row = dt4.transpose(0, 2, 3, 1)
    par = jnp.stack([dt_bias, a_log], axis=0).astype(F32).reshape(2, G, hpg)
    par_col = par.transpose(1, 0, 2)
    par_row = par.transpose(1, 2, 0)
    d_rep = jnp.repeat(d_skip.astype(F32), SSD_HEAD_DIM).reshape(1, d_inner)
    b_off = d_inner // (gps * N)
    return pl.pallas_call(
        _ssd_scan_kernel,
        grid=(batch, G // gps, nc),
        in_specs=[pl.BlockSpec((L, gps * gw), lambda b, g, c: (b * nc + c, g)),
                  pl.BlockSpec((L, gps * N), lambda b, g, c: (b * nc + c, b_off + g)),
                  pl.BlockSpec((L, gps * N), lambda b, g, c: (b * nc + c, b_off + G // gps + g)),
                  pl.BlockSpec((L, gps * gw), lambda b, g, c: (b * nc + c, g)),
                  pl.BlockSpec((1, gps, L, hpg), lambda b, g, c: (b * nc + c, g, 0, 0)),
                  pl.BlockSpec((1, gps, hpg, L), lambda b, g, c: (b * nc + c, g, 0, 0)),
                  pl.BlockSpec((gps, 2, hpg), lambda b, g, c: (g, 0, 0)),
                  pl.BlockSpec((gps, hpg, 2), lambda b, g, c: (g, 0, 0)),
                  pl.BlockSpec((1, gps * gw), lambda b, g, c: (0, g)),
                  pl.BlockSpec((1, gps * gw), lambda b, g, c: (0, g))],
        out_specs=pl.BlockSpec((L, gps * gw), lambda b, g, c: (b * nc + c, g)),
        out_shape=jax.ShapeDtypeStruct((batch * seq, d_inner), BF16),
        scratch_shapes=[pltpu.VMEM((gps, N, gw), F32)],
        compiler_params=_params("parallel", "parallel", "arbitrary"),
        name="ssd_scan",
    )(xbc, xbc, xbc, zx, dt_col, dt_row, par_col, par_row, d_rep, norm_w.reshape(1, d_inner))


def _top16(s):
    r = s.shape[0]
    rows = lax.broadcasted_iota(jnp.int32, s.shape, 0).astype(F32)
    krow = lax.broadcasted_iota(jnp.int32, (PEER_TOPK, s.shape[1]), 0)
    rank = jnp.full(s.shape, PEER_TOPK, jnp.int32)
    vals = jnp.zeros((PEER_TOPK, s.shape[1]), F32)
    for i in range(PEER_TOPK):
        m = jnp.max(s, axis=0, keepdims=True)
        idx = jnp.min(jnp.where(s == m, rows, float(r)), axis=0, keepdims=True)
        hit = rows == idx
        rank = jnp.where(hit, i, rank)
        s = jnp.where(hit, -jnp.inf, s)
        vals = jnp.where(krow == i, m, vals)
    return vals, rank


def _peer_route_kernel(x_ref, nw_ref, wq_ref, keys_ref, xnt_ref, r2_ref, e2_ref, n1_ref, f1_ref):
    x = x_ref[...]
    ms = jnp.mean(x * x, axis=-1, keepdims=True)
    xn = x * lax.rsqrt(ms + RMS_EPS) * nw_ref[...]
    xnt_ref[...] = xn.T.astype(BF16)
    xn = xn.astype(BF16)
    q = _dot(xn, wq_ref[...]).astype(BF16)
    half = keys_ref.shape[-1]
    j8 = lax.broadcasted_iota(jnp.int32, (8, x.shape[0]), 0)
    for h in range(PEER_HEADS):
        s1 = _dot_nt(keys_ref[h, 0], q[:, (2 * h) * half:(2 * h + 1) * half])
        s2 = _dot_nt(keys_ref[h, 1], q[:, (2 * h + 1) * half:(2 * h + 2) * half])
        a, r1 = _top16(s1)
        b, r2 = _top16(s2)
        blocks = [a[0:1] + b]
        for i in range(1, 8):
            blk = a[i:i + 1] + b[0:8]
            if PEER_ROW_LEN[i] < 8:
                blk = jnp.where(j8 < PEER_ROW_LEN[i], blk, -jnp.inf)
            blocks.append(blk)
        blocks.append(a[8:16] + b[0:1])
        cand = jnp.concatenate(blocks, axis=0)
        _, rc = _top16(cand)
        sel = rc < PEER_TOPK
        cmax = a[0:1] + b[0:1]
        z = jnp.sum(jnp.where(sel, jnp.exp(cand - cmax), 0.0), axis=0, keepdims=True)
        self32 = jnp.where(sel, 1.0, 0.0)
        n1 = jnp.zeros(s1.shape, F32)
        off = 0
        for i in range(8):
            ln = 16 if i == 0 else 8
            cnt = jnp.sum(self32[off:off + ln], axis=0, keepdims=True)
            n1 = jnp.where(r1 == i, cnt, n1)
            off += ln
        for i in range(8, 16):
            n1 = jnp.where(r1 == i, self32[off + i - 8:off + i - 7], n1)
        r2_ref[h] = r2.astype(F32).astype(r2_ref.dtype)
        n1_ref[h] = n1
        e2_ref[h] = jnp.where(r2 < PEER_TOPK, jnp.exp(s2 - b[0:1]), 0.0).astype(e2_ref.dtype)
        f1_ref[h] = jnp.where(r1 < PEER_TOPK, jnp.exp(s1 - a[0:1]), 0.0) / z


def peer_route(x, nw, w_q, sub_keys, tm):
    t, d = x.shape
    nk = sub_keys.shape[2]
    fac = jax.ShapeDtypeStruct((PEER_HEADS, nk, t), F32)
    fac16 = jax.ShapeDtypeStruct((PEER_HEADS, nk, t), BF16)
    fspec = pl.BlockSpec((PEER_HEADS, nk, tm), lambda i: (0, 0, i))
    return pl.pallas_call(
        _peer_route_kernel,
        grid=(t // tm,),
        in_specs=[pl.BlockSpec((tm, d), lambda i: (i, 0)),
                  pl.BlockSpec((1, d), lambda i: (0, 0)),
                  pl.BlockSpec(w_q.shape, lambda i: (0, 0)),
                  pl.BlockSpec(sub_keys.shape, lambda i: (0, 0, 0, 0))],
        out_specs=[pl.BlockSpec((d, tm), lambda i: (0, i)), fspec, fspec, fspec, fspec],
        out_shape=[jax.ShapeDtypeStruct((d, t), BF16), fac16, fac16, fac, fac],
        compiler_params=_params("parallel"),
        name="peer_route",
    )(x, nw.reshape(1, d), w_q, sub_keys)


def _gelu(h):
    return 0.5 * h * (1.0 + lax.erf(h * (1.0 / math.sqrt(2.0))))


def _peer_experts_kernel(xnt_ref, u_ref, vt_ref, r2_ref, e2_ref, n1_ref, f1_ref, x_ref, o_ref, acc_ref, g_ref,
                         *, n_keys, sub):
    j = pl.program_id(1)

    @pl.when(j == 0)
    def _():
        acc_ref[...] = jnp.zeros_like(acc_ref)

    eb = u_ref.shape[0]
    m = xnt_ref.shape[1]
    rows16 = 16

    for cc in range(eb // n_keys):
        c = j * (eb // n_keys) + cc
        n1 = [jnp.broadcast_to(n1_ref[h, pl.ds(c, 1), :], (rows16, m)).astype(BF16) for h in range(PEER_HEADS)]
        f1 = [jnp.broadcast_to(f1_ref[h, pl.ds(c, 1), :], (rows16, m)).astype(BF16) for h in range(PEER_HEADS)]
        for k in range(n_keys // rows16):
            rows = slice(k * rows16, (k + 1) * rows16)
            g = jnp.zeros((rows16, m), BF16)
            for h in range(PEER_HEADS):
                g = g + jnp.where(r2_ref[h, rows, :] < n1[h], e2_ref[h, rows, :], jnp.zeros_like(g)) * f1[h]
            g_ref[cc * n_keys + k * rows16:cc * n_keys + (k + 1) * rows16, :] = g

    xnt = xnt_ref[...]
    hts = [_dot(u_ref[sb * sub:(sb + 1) * sub, :], xnt) for sb in range(eb // sub)]
    for sb, ht in enumerate(hts):
        act = g_ref[sb * sub:(sb + 1) * sub, :] * _gelu(ht).astype(BF16)
        acc_ref[...] = _dot(vt_ref[:, sb * sub:(sb + 1) * sub], act) + acc_ref[...]

    @pl.when(j == pl.num_programs(1) - 1)
    def _():
        o_ref[...] = x_ref[...] + acc_ref[...].T


def peer_experts(x, xnt, u, vt, r2, e2, n1, f1, tm, eb, sub):
    t, d = x.shape
    n_exp = u.shape[0]
    n_keys = r2.shape[1]
    fspec = pl.BlockSpec((PEER_HEADS, n_keys, tm), lambda i, j: (0, 0, i))
    return pl.pallas_call(
        functools.partial(_peer_experts_kernel, n_keys=n_keys, sub=sub),
        grid=(t // tm, n_exp // eb),
        in_specs=[pl.BlockSpec((d, tm), lambda i, j: (0, i)),
                  pl.BlockSpec((eb, d), lambda i, j: (j, 0)),
                  pl.BlockSpec((d, eb), lambda i, j: (0, j)),
                  fspec, fspec, fspec, fspec,
                  pl.BlockSpec((tm, d), lambda i, j: (i, 0))],
        out_specs=pl.BlockSpec((tm, d), lambda i, j: (i, 0)),
        out_shape=jax.ShapeDtypeStruct((t, d), F32),
        scratch_shapes=[pltpu.VMEM((d, tm), F32), pltpu.VMEM((eb, tm), BF16)],
        compiler_params=_params("parallel", "arbitrary"),
        name="peer_experts",
    )(xnt, u, vt, r2, e2, n1, f1, x)


def peer_ffn(x, nw, w_q, sub_keys, u, v, route_tm, tm, eb, sub):
    xnt, r2, e2, n1, f1 = peer_route(x, nw, w_q.astype(BF16), sub_keys.astype(BF16), route_tm)
    return peer_experts(x, xnt, u.astype(BF16), v.T.astype(BF16), r2, e2, n1, f1, tm, eb, sub)


def _rmsnorm_kernel(x_ref, nw_ref, o_ref):
    x = x_ref[...]
    ms = jnp.mean(x * x, axis=-1, keepdims=True)
    o_ref[...] = x * lax.rsqrt(ms + RMS_EPS) * nw_ref[...]


def rmsnorm(x, nw, tm):
    t, d = x.shape
    return pl.pallas_call(
        _rmsnorm_kernel,
        grid=(t // tm,),
        in_specs=[pl.BlockSpec((tm, d), lambda i: (i, 0)), pl.BlockSpec((1, d), lambda i: (0, 0))],
        out_specs=pl.BlockSpec((tm, d), lambda i: (i, 0)),
        out_shape=jax.ShapeDtypeStruct((t, d), F32),
        compiler_params=_params("parallel"),
        name="rmsnorm",
    )(x, nw.reshape(1, d))


def _tile(n, pref):
    t = min(n, pref)
    while n % t:
        t -= LANES
    return t


def even_layer(x, nw, w_in, conv_w, w_out, batch, seq):
    t = x.shape[0]
    conv_ch = conv_w.shape[1]
    w_in = w_in.astype(BF16)
    tm = _tile(t, 1024)
    proj_a = norm_matmul(x, nw, w_in[:, :3 * conv_ch], F32, tm, _tile(3 * conv_ch, 512))
    qkv = norm_matmul(x, nw, w_in[:, 3 * conv_ch:], BF16, tm, _tile(w_in.shape[1] - 3 * conv_ch, 512))
    y_conv = even_conv(proj_a, conv_w, batch, seq, _tile(conv_ch, 256))
    y_sb = sb_attention(qkv, batch, seq, _tile(seq, 256))
    w_out = w_out.astype(BF16)
    return matmul_resid([y_conv, y_sb], [w_out[:conv_ch], w_out[conv_ch:]], x, tm, _tile(x.shape[1], 512))


def ssd_layer(x, nw, w_in, conv_w, conv_b, dt_bias, a_log, d_skip, norm_w, w_out, batch, seq):
    t = x.shape[0]
    heads = dt_bias.shape[0]
    d_inner = heads * SSD_HEAD_DIM
    conv_dim = conv_w.shape[1]
    w_in = w_in.astype(BF16)
    tm = _tile(t, 1024)
    zx = norm_matmul(x, nw, w_in[:, :d_inner + conv_dim], F32, tm, _tile(d_inner + conv_dim, 512))
    w_dt = jnp.pad(w_in[:, d_inner + conv_dim:], ((0, 0), (0, LANES - heads)))
    dt_raw = norm_matmul(x, nw, w_dt, F32, tm, LANES)[:, :heads]
    xbc = ssd_conv(zx, conv_w, conv_b, batch, seq, d_inner, _tile(conv_dim, 512))
    y = ssd_scan(zx, xbc, dt_raw, dt_bias, a_log, d_skip, norm_w, batch, seq, 2)
    return matmul_resid([y], [w_out.astype(BF16)], x, tm, _tile(x.shape[1], 512))


def kernel(x, norm_mix, norm_ffn, norm_final, ev_w_in, ev_conv_w, ev_w_out, ssd_w_in, ssd_conv_w, ssd_conv_b, ssd_dt_bias, ssd_a_log, ssd_d, ssd_norm, ssd_w_out, peer_w_q, peer_sub_keys, peer_u, peer_v):
    batch, seq, d = x.shape
    t = batch * seq
    x = x.reshape(t, d)
    depth = norm_mix.shape[0]
    for i in range(depth):
        j = i // 2
        if i % 2 == 0:
            x = even_layer(x, norm_mix[i], ev_w_in[j], ev_conv_w[j], ev_w_out[j], batch, seq)
        else:
            x = ssd_layer(x, norm_mix[i], ssd_w_in[j], ssd_conv_w[j], ssd_conv_b[j], ssd_dt_bias[j],
                          ssd_a_log[j], ssd_d[j], ssd_norm[j], ssd_w_out[j], batch, seq)
        x = peer_ffn(x, norm_ffn[i], peer_w_q[i], peer_sub_keys[i], peer_u[i], peer_v[i],
                     _tile(t, 128), _tile(t, 512), 1024, 512)
    return rmsnorm(x, norm_final, _tile(t, 512)).reshape(batch, seq, d)
```

```python
import functools
import math

import jax
import jax.numpy as jnp
from jax import lax
from jax.experimental import pallas as pl
from jax.experimental.pallas import tpu as pltpu

F32 = jnp.float32
BF16 = jnp.bfloat16

RMS_EPS = 1e-6
LANES = 128
VMEM_LIMIT = 60 * 1024 * 1024

SB_HEADS = 8
SB_HEAD_DIM = 128
SSD_HEAD_DIM = 64
SSD_GROUPS = 8
SSD_STATE = 128
SSD_CHUNK = 128
PEER_HEADS = 8
PEER_KEYS = 128
PEER_TOPK = 16
PEER_ROW_LEN = (16, 8, 5, 4, 3, 2, 2, 2)


def _params(*sem):
    return pltpu.CompilerParams(dimension_semantics=sem, vmem_limit_bytes=VMEM_LIMIT)


def _split3(v):
    hi = v.astype(BF16)
    r = v - hi.astype(F32)
    mid = r.astype(BF16)
    lo = (r - mid.astype(F32)).astype(BF16)
    return hi, mid, lo


def _dot(a, b):
    return jnp.dot(a, b, preferred_element_type=F32)


def _dot_nt(a, b):
    return lax.dot_general(a, b, (((1,), (1,)), ((), ())), preferred_element_type=F32)


def _norm_matmul_kernel(x_ref, nw_ref, w_ref, o_ref, xn_ref):
    @pl.when(pl.program_id(1) == 0)
    def _():
        x = x_ref[...]
        ms = jnp.mean(x * x, axis=-1, keepdims=True)
        xn_ref[...] = (x * lax.rsqrt(ms + RMS_EPS) * nw_ref[...]).astype(BF16)

    o_ref[...] = _dot(xn_ref[...], w_ref[...]).astype(o_ref.dtype)


def norm_matmul(x, nw, w, col0, n, out_dtype, tm, tn):
    m, k = x.shape
    assert col0 % tn == 0 and n % tn == 0 and col0 + n <= w.shape[1]
    cb = col0 // tn
    return pl.pallas_call(
        _norm_matmul_kernel,
        grid=(m // tm, n // tn),
        in_specs=[pl.BlockSpec((tm, k), lambda i, j: (i, 0)),
                  pl.BlockSpec((1, k), lambda i, j: (0, 0)),
                  pl.BlockSpec((k, tn), lambda i, j: (0, cb + j))],
        out_specs=pl.BlockSpec((tm, tn), lambda i, j: (i, j)),
        out_shape=jax.ShapeDtypeStruct((m, n), out_dtype),
        scratch_shapes=[pltpu.VMEM((tm, k), BF16)],
        compiler_params=_params("parallel", "arbitrary"),
        name="norm_matmul",
    )(x, nw.reshape(1, k), w)


def _matmul_resid_kernel(*refs, n_in):
    a_refs = refs[:n_in]
    w_refs = refs[n_in:2 * n_in]
    r_ref = refs[2 * n_in]
    o_ref = refs[2 * n_in + 1]
    acc = r_ref[...]
    for a_ref, w_ref in zip(a_refs, w_refs):
        acc = acc + _dot(a_ref[...], w_ref[...])
    o_ref[...] = acc


def matmul_resid(a_list, w, resid, tm, tn):
    m, n = resid.shape
    n_in = len(a_list)
    ka = a_list[0].shape[1]
    assert all(a.shape[1] == ka for a in a_list) and ka * n_in == w.shape[0]
    in_specs = ([pl.BlockSpec((tm, ka), lambda i, j: (i, 0)) for _ in a_list]
                + [pl.BlockSpec((ka, tn), functools.partial(lambda i, j, r: (r, j), r=r)) for r in range(n_in)]
                + [pl.BlockSpec((tm, tn), lambda i, j: (i, j))])
    return pl.pallas_call(
        functools.partial(_matmul_resid_kernel, n_in=n_in),
        grid=(m // tm, n // tn),
        in_specs=in_specs,
        out_specs=pl.BlockSpec((tm, tn), lambda i, j: (i, j)),
        out_shape=jax.ShapeDtypeStruct((m, n), F32),
        compiler_params=_params("parallel", "arbitrary"),
        name="matmul_resid",
    )(*a_list, *([w] * n_in), resid)


def _causal_conv(u, w_ref):
    k_w = w_ref.shape[0]
    sub = 8
    row = lax.broadcasted_iota(jnp.int32, (sub, u.shape[1]), 0)
    out = u * w_ref[k_w - 1:k_w, :]
    for k in range(k_w - 1):
        shift = k_w - 1 - k
        rolled = pltpu.roll(u, shift, 0)
        top = jnp.where(row >= shift, rolled[:sub], 0.0)
        shifted = jnp.concatenate([top, rolled[sub:]], axis=0)
        out = out + shifted * w_ref[k:k + 1, :]
    return out


def _even_conv_kernel(gb_ref, gc_ref, xv_ref, w_ref, o_ref):
    u = gc_ref[...] * xv_ref[...]
    o_ref[...] = (gb_ref[...] * _causal_conv(u, w_ref)).astype(o_ref.dtype)


def even_conv(proj, conv_w, batch, seq, cb):
    c = conv_w.shape[1]
    nb = c // cb
    return pl.pallas_call(
        _even_conv_kernel,
        grid=(batch, nb),
        in_specs=[pl.BlockSpec((seq, cb), lambda b, j: (b, j)),
                  pl.BlockSpec((seq, cb), lambda b, j: (b, nb + j)),
                  pl.BlockSpec((seq, cb), lambda b, j: (b, 2 * nb + j)),
                  pl.BlockSpec((conv_w.shape[0], cb), lambda b, j: (0, j))],
        out_specs=pl.BlockSpec((seq, cb), lambda b, j: (b, j)),
        out_shape=jax.ShapeDtypeStruct((batch * seq, c), BF16),
        compiler_params=_params("parallel", "parallel"),
        name="even_conv",
    )(proj, proj, proj, conv_w)


def _ssd_conv_kernel(u_ref, w_ref, b_ref, o_ref):
    y = _causal_conv(u_ref[...], w_ref) + b_ref[...]
    o_ref[...] = (y * jax.nn.sigmoid(y)).astype(o_ref.dtype)


def ssd_conv(zx, conv_w, conv_b, batch, seq, col0, cb):
    c = conv_w.shape[1]
    off = col0 // cb
    return pl.pallas_call(
        _ssd_conv_kernel,
        grid=(batch, c // cb),
        in_specs=[pl.BlockSpec((seq, cb), lambda b, j: (b, off + j)),
                  pl.BlockSpec((conv_w.shape[0], cb), lambda b, j: (0, j)),
                  pl.BlockSpec((1, cb), lambda b, j: (0, j))],
        out_specs=pl.BlockSpec((seq, cb), lambda b, j: (b, j)),
        out_shape=jax.ShapeDtypeStruct((batch * seq, c), BF16),
        compiler_params=_params("parallel", "parallel"),
        name="ssd_conv",
    )(zx, conv_w, conv_b.reshape(1, c))


def _sb_attn_kernel(q_ref, k_ref, v_ref, o_ref, z_ref, a_ref, *, blk, scale):
    qi = pl.program_id(2)
    nq = pl.num_programs(2)
    q = q_ref[...]
    w = LANES
    nsub = blk // w
    jj = lax.broadcasted_iota(jnp.int32, (w, 2 * w), 0)
    ss = lax.broadcasted_iota(jnp.int32, (w, 2 * w), 1)
    cum_mat = jnp.where((ss >= w) | (jj > ss), 1.0, 0.0).astype(BF16)
    row = lax.broadcasted_iota(jnp.int32, (blk, blk), 0)
    col = lax.broadcasted_iota(jnp.int32, (blk, blk), 1)

    def rows_of(kb):
        return pl.ds(pl.multiple_of(jnp.clip(kb, 0, nq - 1) * blk, blk), blk)

    def scores(kb, slot):
        z_ref[slot] = _dot_nt(q, k_ref[rows_of(kb), :]) * scale

    def weights(kb, slot, carry):
        z = z_ref[slot]
        limit = jnp.where(kb == qi, row, jnp.where(kb < 0, 0, blk))
        keep = col < limit
        ls = -(jnp.maximum(z, 0.0) + jnp.log(1.0 + jnp.exp(-jnp.abs(z))))
        ls = jnp.where(keep, ls, 0.0)
        hi = ls.astype(BF16)
        lo = (ls - hi.astype(F32)).astype(BF16)
        hi = jnp.concatenate([hi[:, i * w:(i + 1) * w] for i in range(nsub)], axis=0)
        lo = jnp.concatenate([lo[:, i * w:(i + 1) * w] for i in range(nsub)], axis=0)
        cs = _dot(hi, cum_mat) + _dot(lo, cum_mat)
        after = [None] * nsub
        for i in reversed(range(nsub)):
            after[i] = cs[i * blk:(i + 1) * blk, :w] + carry
            carry = carry + cs[i * blk:(i + 1) * blk, w:]
        a = jnp.exp(z + ls + jnp.concatenate(after, axis=1))
        a_ref[slot] = jnp.where(keep, a, 0.0).astype(BF16)
        return carry

    def values(kb, slot, acc):
        return acc + _dot(a_ref[slot], v_ref[rows_of(kb), :])

    a_ref[...] = jnp.zeros_like(a_ref)
    scores(qi, 0)
    scores(qi - 1, 1)

    def body(it, state):
        carry, acc = state
        ka = qi - 2 * it
        acc = values(ka + 2, 0, acc)
        acc = values(ka + 1, 1, acc)
        carry = weights(ka, 0, carry)
        carry = weights(ka - 1, 1, carry)
        scores(ka - 2, 0)
        scores(ka - 3, 1)
        return carry, acc

    n_iter = (qi + 2) // 2
    carry = jnp.zeros((blk, w), F32)
    acc = jnp.zeros((blk, q.shape[1]), F32)
    carry, acc = lax.fori_loop(0, n_iter, body, (carry, acc))
    last = qi - 2 * (n_iter - 1)
    acc = values(last, 0, acc)
    acc = values(last - 1, 1, acc)
    o_ref[...] = acc.astype(o_ref.dtype)


def sb_attention(qkv, batch, seq, blk):
    d = SB_HEAD_DIM
    nq = seq // blk
    return pl.pallas_call(
        functools.partial(_sb_attn_kernel, blk=blk, scale=d ** -0.5),
        grid=(batch, SB_HEADS, nq),
        in_specs=[pl.BlockSpec((blk, d), lambda b, h, i: (b * nq + i, h)),
                  pl.BlockSpec((seq, d), lambda b, h, i: (b, SB_HEADS + h)),
                  pl.BlockSpec((seq, d), lambda b, h, i: (b, 2 * SB_HEADS + h))],
        out_specs=pl.BlockSpec((blk, d), lambda b, h, i: (b * nq + i, h)),
        out_shape=jax.ShapeDtypeStruct((batch * seq, SB_HEADS * d), BF16),
        scratch_shapes=[pltpu.VMEM((2, blk, blk), F32), pltpu.VMEM((2, blk, blk), BF16)],
        compiler_params=_params("parallel", "parallel", "arbitrary"),
        name="sb_attention",
    )(qkv, qkv, qkv)


def _softplus(x):
    return jnp.maximum(x, 0.0) + jnp.log1p(jnp.exp(-jnp.abs(x)))


def _onehot3(k, n, width):
    rr = lax.broadcasted_iota(jnp.int32, (3 * k, n), 0) % k
    cc = lax.broadcasted_iota(jnp.int32, (3 * k, n), 1) // width
    return jnp.where(rr == cc, 1.0, 0.0).astype(BF16)


def _expand(v, mat3):
    return _dot(jnp.concatenate(_split3(v), axis=1), mat3)


def _ssd_group(x, b, c, z, dt_c, dt_r, bias_c, alog_c, bias_r, alog_r, d_rep, nw, st):
    L, gw = x.shape
    xb = x
    x = x.astype(F32)
    hpg = dt_c.shape[-1]
    P = SSD_HEAD_DIM
    dt_c = _softplus(dt_c + bias_c)
    dt_r = _softplus(dt_r + bias_r)
    ti = lax.broadcasted_iota(jnp.int32, (L, L), 0)
    si = lax.broadcasted_iota(jnp.int32, (L, L), 1)
    tri = jnp.where(si <= ti, 1.0, 0.0).astype(BF16)
    tri_t3 = jnp.concatenate([jnp.where(ti <= si, 1.0, 0.0).astype(BF16)] * 3, axis=0)
    h0, h1, h2 = _split3(dt_c * -jnp.exp(alog_c))
    acum_c = _dot(tri, h0) + _dot(tri, h1) + _dot(tri, h2)
    acum_r = _expand(dt_r * -jnp.exp(alog_r), tri_t3)

    bq = b.astype(BF16)
    cq = c.astype(BF16)
    cb = _dot_nt(cq, bq)
    seg = _expand(acum_c, _onehot3(hpg, hpg * L, L)) - jnp.concatenate(
        [jnp.broadcast_to(acum_r[h:h + 1, :], (L, L)) for h in range(hpg)], axis=1)
    causal = jnp.concatenate([si <= ti] * hpg, axis=1)
    dt_s = jnp.concatenate([jnp.broadcast_to(dt_r[h:h + 1, :], (L, L)) for h in range(hpg)], axis=1)
    m = jnp.concatenate([cb] * hpg, axis=1) * jnp.exp(jnp.where(causal, seg, -jnp.inf)) * dt_s
    ch_head = lax.broadcasted_iota(jnp.int32, (L, gw), 1) // P
    x_bd = jnp.concatenate([jnp.where(ch_head == h, xb, jnp.zeros_like(xb)) for h in range(hpg)], axis=0)
    y_intra = _dot(m.astype(BF16), x_bd)

    ew = jnp.concatenate([jnp.exp(acum_c), jnp.exp(acum_c[L - 1:L, :] - acum_c) * dt_c], axis=1)
    ew = _expand(ew, _onehot3(2 * hpg, 2 * gw, P))
    e_ch = ew[:, :gw]
    w_ch = ew[:, gw:]
    y = y_intra + _dot(cq, st.astype(BF16)) * e_ch + x * d_rep
    upd = _dot(b.astype(F32).T.astype(BF16), (x * w_ch).astype(BF16))
    st = st * e_ch[L - 1:L, :] + upd

    yg = y * (z * jax.nn.sigmoid(z))
    ms = jnp.mean(yg * yg, axis=-1, keepdims=True)
    return yg * lax.rsqrt(ms + RMS_EPS) * nw, st


def _ssd_scan_kernel(x_ref, b_ref, c_ref, z_ref, dtc_ref, dtr_ref, pc_ref, pr_ref, drep_ref, nw_ref,
                     o_ref, st_ref):
    n_state = b_ref.shape[1] // st_ref.shape[0]
    gw = x_ref.shape[1] // st_ref.shape[0]

    @pl.when(pl.program_id(2) == 0)
    def _():
        st_ref[...] = jnp.zeros_like(st_ref)

    for g in range(st_ref.shape[0]):
        ch = slice(g * gw, (g + 1) * gw)
        ns = slice(g * n_state, (g + 1) * n_state)
        y, st = _ssd_group(x_ref[:, ch], b_ref[:, ns], c_ref[:, ns], z_ref[:, ch], dtc_ref[0, g], dtr_ref[0, g],
                           pc_ref[g, 0:1, :], pc_ref[g, 1:2, :], pr_ref[g, :, 0:1], pr_ref[g, :, 1:2],
                           drep_ref[:, ch], nw_ref[:, ch], st_ref[g])
        st_ref[g] = st
        o_ref[:, ch] = y.astype(o_ref.dtype)


def ssd_scan(zx, xbc, dt_raw, dt_bias, a_log, d_skip, norm_w, batch, seq, gps):
    L = SSD_CHUNK
    G = SSD_GROUPS
    N = SSD_STATE
    heads = dt_raw.shape[1]
    hpg = heads // G
    gw = hpg * SSD_HEAD_DIM
    d_inner = heads * SSD_HEAD_DIM
    nc = seq // L
    n_chunks = batch * nc
    dt4 = dt_raw.reshape(n_chunks, L, G, hpg)
    dt_col = dt4.transpose(0, 2, 1, 3)
    dt_row = dt4.transpose(0, 2, 3, 1)
    par = jnp.stack([dt_bias, a_log], axis=0).astype(F32).reshape(2, G, hpg)
    par_col = par.transpose(1, 0, 2)
    par_row = par.transpose(1, 2, 0)
    d_rep = jnp.repeat(d_skip.astype(F32), SSD_HEAD_DIM).reshape(1, d_inner)
    b_off = d_inner // (gps * N)
    return pl.pallas_call(
        _ssd_scan_kernel,
        grid=(batch, G // gps, nc),
        in_specs=[pl.BlockSpec((L, gps * gw), lambda b, g, c: (b * nc + c, g)),
                  pl.BlockSpec((L, gps * N), lambda b, g, c: (b * nc + c, b_off + g)),
                  pl.BlockSpec((L, gps * N), lambda b, g, c: (b * nc + c, b_off + G // gps + g)),
                  pl.BlockSpec((L, gps * gw), lambda b, g, c: (b * nc + c, g)),
                  pl.BlockSpec((1, gps, L, hpg), lambda b, g, c: (b * nc + c, g, 0, 0)),
                  pl.BlockSpec((1, gps, hpg, L), lambda b, g, c: (b * nc + c, g, 0, 0)),
                  pl.BlockSpec((gps, 2, hpg), lambda b, g, c: (g, 0, 0)),
                  pl.BlockSpec((gps, hpg, 2), lambda b, g, c: (g, 0, 0)),
                  pl.BlockSpec((1, gps * gw), lambda b, g, c: (0, g)),
                  pl.BlockSpec((1, gps * gw), lambda b, g, c: (0, g))],
        out_specs=pl.BlockSpec((L, gps * gw), lambda b, g, c: (b * nc + c, g)),
        out_shape=jax.ShapeDtypeStruct((batch * seq, d_inner), BF16),
        scratch_shapes=[pltpu.VMEM((gps, N, gw), F32)],
        compiler_params=_params("parallel", "parallel", "arbitrary"),
        name="ssd_scan",
    )(xbc, xbc, xbc, zx, dt_col, dt_row, par_col, par_row, d_rep, norm_w.reshape(1, d_inner))


def _top16(s):
    r = s.shape[0]
    rows = lax.broadcasted_iota(jnp.int32, s.shape, 0).astype(F32)
    krow = lax.broadcasted_iota(jnp.int32, (PEER_TOPK, s.shape[1]), 0)
    rank = jnp.full(s.shape, PEER_TOPK, jnp.int32)
    vals = jnp.zeros((PEER_TOPK, s.shape[1]), F32)
    for i in range(PEER_TOPK):
        m = jnp.max(s, axis=0, keepdims=True)
        idx = jnp.min(jnp.where(s == m, rows, float(r)), axis=0, keepdims=True)
        hit = rows == idx
        rank = jnp.where(hit, i, rank)
        s = jnp.where(hit, -jnp.inf, s)
        vals = jnp.where(krow == i, m, vals)
    return vals, rank, None


def _top16_distinct(s):
    krow = lax.broadcasted_iota(jnp.int32, (PEER_TOPK, s.shape[1]), 0)
    rank = jnp.full(s.shape, PEER_TOPK, jnp.int32)
    vals = jnp.zeros((PEER_TOPK, s.shape[1]), F32)
    for i in range(PEER_TOPK):
        m = jnp.max(s, axis=0, keepdims=True)
        hit = s == m
        rank = jnp.where(hit, i, rank)
        s = jnp.where(hit, -jnp.inf, s)
        vals = jnp.where(krow == i, m, vals)
    taken = jnp.sum(jnp.where(rank < PEER_TOPK, 1.0, 0.0), axis=0, keepdims=True)
    return vals, rank, jnp.where(taken == PEER_TOPK, 1.0, 0.0)


def _peer_route_kernel(x_ref, nw_ref, wq_ref, keys_ref, xnt_ref, r2_ref, e2_ref, n1_ref, f1_ref):
    x = x_ref[...]
    ms = jnp.mean(x * x, axis=-1, keepdims=True)
    xn = x * lax.rsqrt(ms + RMS_EPS) * nw_ref[...]
    xnt_ref[...] = xn.T.astype(BF16)
    xn = xn.astype(BF16)
    q = _dot(xn, wq_ref[...]).astype(BF16)

    ok = _peer_route_heads(q, keys_ref, r2_ref, e2_ref, n1_ref, f1_ref, _top16_distinct)

    @pl.when(jnp.min(ok) < 1.0)
    def _():
        _peer_route_heads(q, keys_ref, r2_ref, e2_ref, n1_ref, f1_ref, _top16)


def _peer_route_heads(q, keys_ref, r2_ref, e2_ref, n1_ref, f1_ref, top16):
    half = keys_ref.shape[-1]
    j8 = lax.broadcasted_iota(jnp.int32, (8, q.shape[0]), 0)
    ok_all = jnp.ones((1, q.shape[0]), F32)
    for h in range(PEER_HEADS):
        s1 = _dot_nt(keys_ref[h, 0], q[:, (2 * h) * half:(2 * h + 1) * half])
        s2 = _dot_nt(keys_ref[h, 1], q[:, (2 * h + 1) * half:(2 * h + 2) * half])
        a, r1, ok1 = top16(s1)
        b, r2, ok2 = top16(s2)
        blocks = [a[0:1] + b]
        for i in range(1, 8):
            blk = a[i:i + 1] + b[0:8]
            if PEER_ROW_LEN[i] < 8:
                blk = jnp.where(j8 < PEER_ROW_LEN[i], blk, -jnp.inf)
            blocks.append(blk)
        blocks.append(a[8:16] + b[0:1])
        cand = jnp.concatenate(blocks, axis=0)
        _, rc, ok3 = top16(cand)
        for ok in (ok1, ok2, ok3):
            if ok is not None:
                ok_all = jnp.minimum(ok_all, ok)
        sel = rc < PEER_TOPK
        cmax = a[0:1] + b[0:1]
        z = jnp.sum(jnp.where(sel, jnp.exp(cand - cmax), 0.0), axis=0, keepdims=True)
        self32 = jnp.where(sel, 1.0, 0.0)
        n1 = jnp.zeros(s1.shape, F32)
        off = 0
        for i in range(8):
            ln = 16 if i == 0 else 8
            cnt = jnp.sum(self32[off:off + ln], axis=0, keepdims=True)
            n1 = jnp.where(r1 == i, cnt, n1)
            off += ln
        for i in range(8, 16):
            n1 = jnp.where(r1 == i, self32[off + i - 8:off + i - 7], n1)
        r2_ref[h] = r2.astype(F32).astype(r2_ref.dtype)
        n1_ref[h] = n1
        e2_ref[h] = jnp.where(r2 < PEER_TOPK, jnp.exp(s2 - b[0:1]), 0.0).astype(e2_ref.dtype)
        f1_ref[h] = jnp.where(r1 < PEER_TOPK, jnp.exp(s1 - a[0:1]), 0.0) / z
    return ok_all


def peer_route(x, nw, w_q, sub_keys, tm):
    t, d = x.shape
    nk = sub_keys.shape[2]
    fac = jax.ShapeDtypeStruct((PEER_HEADS, nk, t), F32)
    fac16 = jax.ShapeDtypeStruct((PEER_HEADS, nk, t), BF16)
    fspec = pl.BlockSpec((PEER_HEADS, nk, tm), lambda i: (0, 0, i))
    return pl.pallas_call(
        _peer_route_kernel,
        grid=(t // tm,),
        in_specs=[pl.BlockSpec((tm, d), lambda i: (i, 0)),
                  pl.BlockSpec((1, d), lambda i: (0, 0)),
                  pl.BlockSpec(w_q.shape, lambda i: (0, 0)),
                  pl.BlockSpec(sub_keys.shape, lambda i: (0, 0, 0, 0))],
        out_specs=[pl.BlockSpec((d, tm), lambda i: (0, i)), fspec, fspec, fspec, fspec],
        out_shape=[jax.ShapeDtypeStruct((d, t), BF16), fac16, fac16, fac, fac],
        compiler_params=_params("parallel"),
        name="peer_route",
    )(x, nw.reshape(1, d), w_q, sub_keys)


def _gelu(h):
    return 0.5 * h * (1.0 + lax.erf(h * (1.0 / math.sqrt(2.0))))


def _peer_experts_kernel(xnt_ref, u_ref, vt_ref, r2_ref, e2_ref, n1_ref, f1_ref, x_ref, o_ref, acc_ref, g_ref,
                         *, n_keys, sub):
    j = pl.program_id(1)

    @pl.when(j == 0)
    def _():
        acc_ref[...] = jnp.zeros_like(acc_ref)

    eb = u_ref.shape[0]
    m = xnt_ref.shape[1]
    rows16 = 16

    for cc in range(eb // n_keys):
        c = j * (eb // n_keys) + cc
        n1 = [jnp.broadcast_to(n1_ref[h, pl.ds(c, 1), :], (rows16, m)).astype(BF16) for h in range(PEER_HEADS)]
        f1 = [jnp.broadcast_to(f1_ref[h, pl.ds(c, 1), :], (rows16, m)).astype(BF16) for h in range(PEER_HEADS)]
        for k in range(n_keys // rows16):
            rows = slice(k * rows16, (k + 1) * rows16)
            g = jnp.zeros((rows16, m), BF16)
            for h in range(PEER_HEADS):
                g = g + jnp.where(r2_ref[h, rows, :] < n1[h], e2_ref[h, rows, :], jnp.zeros_like(g)) * f1[h]
            g_ref[cc * n_keys + k * rows16:cc * n_keys + (k + 1) * rows16, :] = g

    xnt = xnt_ref[...]
    hts = [_dot(u_ref[sb * sub:(sb + 1) * sub, :], xnt) for sb in range(eb // sub)]
    for sb, ht in enumerate(hts):
        act = g_ref[sb * sub:(sb + 1) * sub, :] * _gelu(ht).astype(BF16)
        acc_ref[...] = _dot(vt_ref[:, sb * sub:(sb + 1) * sub], act) + acc_ref[...]

    @pl.when(j == pl.num_programs(1) - 1)
    def _():
        o_ref[...] = x_ref[...] + acc_ref[...].T


def peer_experts(x, xnt, u, vt, r2, e2, n1, f1, tm, eb, sub):
    t, d = x.shape
    n_exp = u.shape[0]
    n_keys = r2.shape[1]
    fspec = pl.BlockSpec((PEER_HEADS, n_keys, tm), lambda i, j: (0, 0, i))
    return pl.pallas_call(
        functools.partial(_peer_experts_kernel, n_keys=n_keys, sub=sub),
        grid=(t // tm, n_exp // eb),
        in_specs=[pl.BlockSpec((d, tm), lambda i, j: (0, i)),
                  pl.BlockSpec((eb, d), lambda i, j: (j, 0)),
                  pl.BlockSpec((d, eb), lambda i, j: (0, j)),
                  fspec, fspec, fspec, fspec,
                  pl.BlockSpec((tm, d), lambda i, j: (i, 0))],
        out_specs=pl.BlockSpec((tm, d), lambda i, j: (i, 0)),
        out_shape=jax.ShapeDtypeStruct((t, d), F32),
        scratch_shapes=[pltpu.VMEM((d, tm), F32), pltpu.VMEM((eb, tm), BF16)],
        compiler_params=_params("parallel", "arbitrary"),
        name="peer_experts",
    )(xnt, u, vt, r2, e2, n1, f1, x)


def peer_ffn(x, nw, w_q, sub_keys, u, v, route_tm, tm, eb, sub):
    xnt, r2, e2, n1, f1 = peer_route(x, nw, w_q.astype(BF16), sub_keys.astype(BF16), route_tm)
    return peer_experts(x, xnt, u.astype(BF16), v.T.astype(BF16), r2, e2, n1, f1, tm, eb, sub)


def _rmsnorm_kernel(x_ref, nw_ref, o_ref):
    x = x_ref[...]
    ms = jnp.mean(x * x, axis=-1, keepdims=True)
    o_ref[...] = x * lax.rsqrt(ms + RMS_EPS) * nw_ref[...]


def rmsnorm(x, nw, tm):
    t, d = x.shape
    return pl.pallas_call(
        _rmsnorm_kernel,
        grid=(t // tm,),
        in_specs=[pl.BlockSpec((tm, d), lambda i: (i, 0)), pl.BlockSpec((1, d), lambda i: (0, 0))],
        out_specs=pl.BlockSpec((tm, d), lambda i: (i, 0)),
        out_shape=jax.ShapeDtypeStruct((t, d), F32),
        compiler_params=_params("parallel"),
        name="rmsnorm",
    )(x, nw.reshape(1, d))


def _tile(n, pref):
    t = min(n, pref)
    while n % t:
        t -= LANES
    return t


def even_layer(x, nw, w_in, conv_w, w_out, batch, seq):
    t = x.shape[0]
    conv_ch = conv_w.shape[1]
    w_in = w_in.astype(BF16)
    tm = _tile(t, 1024)
    n_a = 3 * conv_ch
    n_b = w_in.shape[1] - n_a
    proj_a = norm_matmul(x, nw, w_in, 0, n_a, F32, tm, _tile(n_a, 1024))
    qkv = norm_matmul(x, nw, w_in, n_a, n_b, BF16, tm, _tile(math.gcd(n_a, n_b), 1024))
    y_conv = even_conv(proj_a, conv_w, batch, seq, _tile(conv_ch, 256))
    y_sb = sb_attention(qkv, batch, seq, _tile(seq, 256))
    return matmul_resid([y_conv, y_sb], w_out.astype(BF16), x, tm, _tile(x.shape[1], 1024))


def ssd_layer(x, nw, w_in, conv_w, conv_b, dt_bias, a_log, d_skip, norm_w, w_out, batch, seq):
    t = x.shape[0]
    heads = dt_bias.shape[0]
    d_inner = heads * SSD_HEAD_DIM
    conv_dim = conv_w.shape[1]
    w_in = w_in.astype(BF16)
    tm = _tile(t, 1024)
    zx = norm_matmul(x, nw, w_in, 0, d_inner + conv_dim, F32, tm, _tile(d_inner + conv_dim, 1024))
    w_dt = jnp.pad(w_in[:, d_inner + conv_dim:], ((0, 0), (0, LANES - heads)))
    dt_raw = norm_matmul(x, nw, w_dt, 0, LANES, F32, tm, LANES)[:, :heads]
    xbc = ssd_conv(zx, conv_w, conv_b, batch, seq, d_inner, _tile(conv_dim, 512))
    y = ssd_scan(zx, xbc, dt_raw, dt_bias, a_log, d_skip, norm_w, batch, seq, 4)
    return matmul_resid([y], w_out.astype(BF16), x, tm, _tile(x.shape[1], 512))


def kernel(x, norm_mix, norm_ffn, norm_final, ev_w_in, ev_conv_w, ev_w_out, ssd_w_in, ssd_conv_w, ssd_conv_b, ssd_dt_bias, ssd_a_log, ssd_d, ssd_norm, ssd_w_out, peer_w_q, peer_sub_keys, peer_u, peer_v):
    batch, seq, d = x.shape
    t = batch * seq
    x = x.reshape(t, d)
    depth = norm_mix.shape[0]
    for i in range(depth):
        j = i // 2
        if i % 2 == 0:
            x = even_layer(x, norm_mix[i], ev_w_in[j], ev_conv_w[j], ev_w_out[j], batch, seq)
        else:
            x = ssd_layer(x, norm_mix[i], ssd_w_in[j], ssd_conv_w[j], ssd_conv_b[j], ssd_dt_bias[j],
                          ssd_a_log[j], ssd_d[j], ssd_norm[j], ssd_w_out[j], batch, seq)
        x = peer_ffn(x, norm_ffn[i], peer_w_q[i], peer_sub_keys[i], peer_u[i], peer_v[i],
                     _tile(t, 128), _tile(t, 512), 1024, 256)
    return rmsnorm(x, norm_final, _tile(t, 512)).reshape(batch, seq, d)
```

```python
import functools
import math

import jax
import jax.numpy as jnp
from jax import lax
from jax.experimental import pallas as pl
from jax.experimental.pallas import tpu as pltpu

F32 = jnp.float32
BF16 = jnp.bfloat16

RMS_EPS = 1e-6
LANES = 128
VMEM_LIMIT = 60 * 1024 * 1024

SB_HEADS = 8
SB_HEAD_DIM = 128
SB_DROPPED = -1e30
SSD_HEAD_DIM = 64
SSD_GROUPS = 8
SSD_STATE = 128
SSD_CHUNK = 128
PEER_HEADS = 8
PEER_KEYS = 128
PEER_TOPK = 16
PEER_ROW_LEN = (16, 8, 5, 4, 3, 2, 2, 2)


def _params(*sem):
    return pltpu.CompilerParams(dimension_semantics=sem, vmem_limit_bytes=VMEM_LIMIT)


def _split3(v):
    hi = v.astype(BF16)
    r = v - hi.astype(F32)
    mid = r.astype(BF16)
    lo = (r - mid.astype(F32)).astype(BF16)
    return hi, mid, lo


def _dot(a, b):
    return jnp.dot(a, b, preferred_element_type=F32)


def _dot_nt(a, b):
    return lax.dot_general(a, b, (((1,), (1,)), ((), ())), preferred_element_type=F32)


def _norm_matmul_kernel(x_ref, nw_ref, w_ref, o_ref, xn_ref):
    @pl.when(pl.program_id(1) == 0)
    def _():
        x = x_ref[...]
        ms = jnp.mean(x * x, axis=-1, keepdims=True)
        xn_ref[...] = (x * lax.rsqrt(ms + RMS_EPS) * nw_ref[...]).astype(BF16)

    o_ref[...] = _dot(xn_ref[...], w_ref[...]).astype(o_ref.dtype)


def norm_matmul(x, nw, w, col0, n, out_dtype, tm, tn):
    m, k = x.shape
    assert col0 % tn == 0 and n % tn == 0 and col0 + n <= w.shape[1]
    cb = col0 // tn
    return pl.pallas_call(
        _norm_matmul_kernel,
        grid=(m // tm, n // tn),
        in_specs=[pl.BlockSpec((tm, k), lambda i, j: (i, 0)),
                  pl.BlockSpec((1, k), lambda i, j: (0, 0)),
                  pl.BlockSpec((k, tn), lambda i, j: (0, cb + j))],
        out_specs=pl.BlockSpec((tm, tn), lambda i, j: (i, j)),
        out_shape=jax.ShapeDtypeStruct((m, n), out_dtype),
        scratch_shapes=[pltpu.VMEM((tm, k), BF16)],
        compiler_params=_params("parallel", "arbitrary"),
        name="norm_matmul",
    )(x, nw.reshape(1, k), w)


def _matmul_resid_kernel(*refs, n_in):
    a_refs = refs[:n_in]
    w_refs = refs[n_in:2 * n_in]
    r_ref = refs[2 * n_in]
    o_ref = refs[2 * n_in + 1]
    acc = r_ref[...]
    for a_ref, w_ref in zip(a_refs, w_refs):
        acc = acc + _dot(a_ref[...], w_ref[...])
    o_ref[...] = acc


def matmul_resid(a_list, w, resid, tm, tn):
    m, n = resid.shape
    n_in = len(a_list)
    ka = a_list[0].shape[1]
    assert all(a.shape[1] == ka for a in a_list) and ka * n_in == w.shape[0]
    in_specs = ([pl.BlockSpec((tm, ka), lambda i, j: (i, 0)) for _ in a_list]
                + [pl.BlockSpec((ka, tn), functools.partial(lambda i, j, r: (r, j), r=r)) for r in range(n_in)]
                + [pl.BlockSpec((tm, tn), lambda i, j: (i, j))])
    return pl.pallas_call(
        functools.partial(_matmul_resid_kernel, n_in=n_in),
        grid=(m // tm, n // tn),
        in_specs=in_specs,
        out_specs=pl.BlockSpec((tm, tn), lambda i, j: (i, j)),
        out_shape=jax.ShapeDtypeStruct((m, n), F32),
        compiler_params=_params("parallel", "arbitrary"),
        name="matmul_resid",
    )(*a_list, *([w] * n_in), resid)


def _causal_conv(u, w_ref):
    k_w = w_ref.shape[0]
    sub = 8
    row = lax.broadcasted_iota(jnp.int32, (sub, u.shape[1]), 0)
    out = u * w_ref[k_w - 1:k_w, :]
    for k in range(k_w - 1):
        shift = k_w - 1 - k
        rolled = pltpu.roll(u, shift, 0)
        top = jnp.where(row >= shift, rolled[:sub], 0.0)
        shifted = jnp.concatenate([top, rolled[sub:]], axis=0)
        out = out + shifted * w_ref[k:k + 1, :]
    return out


def _even_conv_kernel(gb_ref, gc_ref, xv_ref, w_ref, o_ref):
    u = gc_ref[...] * xv_ref[...]
    o_ref[...] = (gb_ref[...] * _causal_conv(u, w_ref)).astype(o_ref.dtype)


def even_conv(proj, conv_w, batch, seq, cb):
    c = conv_w.shape[1]
    nb = c // cb
    return pl.pallas_call(
        _even_conv_kernel,
        grid=(batch, nb),
        in_specs=[pl.BlockSpec((seq, cb), lambda b, j: (b, j)),
                  pl.BlockSpec((seq, cb), lambda b, j: (b, nb + j)),
                  pl.BlockSpec((seq, cb), lambda b, j: (b, 2 * nb + j)),
                  pl.BlockSpec((conv_w.shape[0], cb), lambda b, j: (0, j))],
        out_specs=pl.BlockSpec((seq, cb), lambda b, j: (b, j)),
        out_shape=jax.ShapeDtypeStruct((batch * seq, c), BF16),
        compiler_params=_params("parallel", "parallel"),
        name="even_conv",
    )(proj, proj, proj, conv_w)


def _ssd_conv_kernel(u_ref, w_ref, b_ref, o_ref):
    y = _causal_conv(u_ref[...], w_ref) + b_ref[...]
    o_ref[...] = (y * jax.nn.sigmoid(y)).astype(o_ref.dtype)


def ssd_conv(zx, conv_w, conv_b, batch, seq, col0, cb):
    c = conv_w.shape[1]
    off = col0 // cb
    return pl.pallas_call(
        _ssd_conv_kernel,
        grid=(batch, c // cb),
        in_specs=[pl.BlockSpec((seq, cb), lambda b, j: (b, off + j)),
                  pl.BlockSpec((conv_w.shape[0], cb), lambda b, j: (0, j)),
                  pl.BlockSpec((1, cb), lambda b, j: (0, j))],
        out_specs=pl.BlockSpec((seq, cb), lambda b, j: (b, j)),
        out_shape=jax.ShapeDtypeStruct((batch * seq, c), BF16),
        compiler_params=_params("parallel", "parallel"),
        name="ssd_conv",
    )(zx, conv_w, conv_b.reshape(1, c))


def _sb_attn_kernel(q_ref, k_ref, v_ref, o_ref, z_ref, a_ref, *, blk, scale):
    qi = pl.program_id(2)
    nq = pl.num_programs(2)
    d = SB_HEAD_DIM
    heads = range(q_ref.shape[1] // d)
    w = LANES
    nsub = blk // w
    jj = lax.broadcasted_iota(jnp.int32, (w, 2 * w), 0)
    ss = lax.broadcasted_iota(jnp.int32, (w, 2 * w), 1)
    cum_mat = jnp.where((ss >= w) | (jj > ss), 1.0, 0.0).astype(BF16)
    row = lax.broadcasted_iota(jnp.int32, (blk, blk), 0)
    col = lax.broadcasted_iota(jnp.int32, (blk, blk), 1)

    def rows_of(kb):
        return pl.ds(pl.multiple_of(jnp.clip(kb, 0, nq - 1) * blk, blk), blk)

    def scores(kb, h, slot):
        hd = slice(h * d, (h + 1) * d)
        z_ref[2 * h + slot] = _dot_nt(q_ref[:, hd], k_ref[rows_of(kb), hd]) * scale

    def weights(kb, h, slot, carry):
        limit = jnp.where(kb == qi, row, jnp.where(kb < 0, 0, blk))
        z = jnp.where(col < limit, z_ref[2 * h + slot], SB_DROPPED)
        log_beta = jnp.minimum(z, 0.0) - jnp.log(1.0 + jnp.exp(-jnp.abs(z)))
        ls = log_beta - z
        hi = ls.astype(BF16)
        lo = (ls - hi.astype(F32)).astype(BF16)
        hi = jnp.concatenate([hi[:, i * w:(i + 1) * w] for i in range(nsub)], axis=0)
        lo = jnp.concatenate([lo[:, i * w:(i + 1) * w] for i in range(nsub)], axis=0)
        cs = _dot(hi, cum_mat) + _dot(lo, cum_mat)
        after = [None] * nsub
        for i in reversed(range(nsub)):
            after[i] = cs[i * blk:(i + 1) * blk, :w] + carry
            carry = carry + cs[i * blk:(i + 1) * blk, w:]
        a_ref[2 * h + slot] = jnp.exp(log_beta + jnp.concatenate(after, axis=1)).astype(BF16)
        return carry

    def values(kb, h, slot, acc):
        return acc + _dot(a_ref[2 * h + slot], v_ref[rows_of(kb), h * d:(h + 1) * d])

    a_ref[...] = jnp.zeros_like(a_ref)
    for h in heads:
        scores(qi, h, 0)
        scores(qi - 1, h, 1)

    def body(it, state):
        carry, acc = state
        ka = qi - 2 * it
        acc = tuple(values(ka + 1, h, 1, values(ka + 2, h, 0, acc[h])) for h in heads)
        carry = tuple(weights(ka - 1, h, 1, weights(ka, h, 0, carry[h])) for h in heads)
        for h in heads:
            scores(ka - 2, h, 0)
            scores(ka - 3, h, 1)
        return carry, acc

    n_iter = (qi + 2) // 2
    carry = tuple(jnp.zeros((blk, w), F32) for _ in heads)
    acc = tuple(jnp.zeros((blk, d), F32) for _ in heads)
    carry, acc = lax.fori_loop(0, n_iter, body, (carry, acc))
    last = qi - 2 * (n_iter - 1)
    for h in heads:
        o_ref[:, h * d:(h + 1) * d] = values(last - 1, h, 1, values(last, h, 0, acc[h])).astype(o_ref.dtype)


def sb_attention(qkv, batch, seq, blk, hps):
    d = SB_HEAD_DIM
    nq = seq // blk
    hg = SB_HEADS // hps
    return pl.pallas_call(
        functools.partial(_sb_attn_kernel, blk=blk, scale=d ** -0.5),
        grid=(batch, hg, nq),
        in_specs=[pl.BlockSpec((blk, hps * d), lambda b, h, i: (b * nq + i, h)),
                  pl.BlockSpec((seq, hps * d), lambda b, h, i: (b, hg + h)),
                  pl.BlockSpec((seq, hps * d), lambda b, h, i: (b, 2 * hg + h))],
        out_specs=pl.BlockSpec((blk, hps * d), lambda b, h, i: (b * nq + i, h)),
        out_shape=jax.ShapeDtypeStruct((batch * seq, SB_HEADS * d), BF16),
        scratch_shapes=[pltpu.VMEM((2 * hps, blk, blk), F32), pltpu.VMEM((2 * hps, blk, blk), BF16)],
        compiler_params=_params("parallel", "parallel", "arbitrary"),
        name="sb_attention",
    )(qkv, qkv, qkv)


def _softplus(x):
    return jnp.maximum(x, 0.0) + jnp.log1p(jnp.exp(-jnp.abs(x)))


def _onehot3(k, n, width):
    rr = lax.broadcasted_iota(jnp.int32, (3 * k, n), 0) % k
    cc = lax.broadcasted_iota(jnp.int32, (3 * k, n), 1) // width
    return jnp.where(rr == cc, 1.0, 0.0).astype(BF16)


def _expand(v, mat3):
    return _dot(jnp.concatenate(_split3(v), axis=1), mat3)


def _ssd_group(x, b, c, z, dt_c, dt_r, bias_c, alog_c, bias_r, alog_r, d_rep, nw, st):
    L, gw = x.shape
    xb = x
    x = x.astype(F32)
    hpg = dt_c.shape[-1]
    P = SSD_HEAD_DIM
    dt_c = _softplus(dt_c + bias_c)
    dt_r = _softplus(dt_r + bias_r)
    ti = lax.broadcasted_iota(jnp.int32, (L, L), 0)
    si = lax.broadcasted_iota(jnp.int32, (L, L), 1)
    tri = jnp.where(si <= ti, 1.0, 0.0).astype(BF16)
    tri_t3 = jnp.concatenate([jnp.where(ti <= si, 1.0, 0.0).astype(BF16)] * 3, axis=0)
    h0, h1, h2 = _split3(dt_c * -jnp.exp(alog_c))
    acum_c = _dot(tri, h0) + _dot(tri, h1) + _dot(tri, h2)
    acum_r = _expand(dt_r * -jnp.exp(alog_r), tri_t3)

    bq = b.astype(BF16)
    cq = c.astype(BF16)
    cb = _dot_nt(cq, bq)
    seg = _expand(acum_c, _onehot3(hpg, hpg * L, L)) - jnp.concatenate(
        [jnp.broadcast_to(acum_r[h:h + 1, :], (L, L)) for h in range(hpg)], axis=1)
    causal = jnp.concatenate([si <= ti] * hpg, axis=1)
    dt_s = jnp.concatenate([jnp.broadcast_to(dt_r[h:h + 1, :], (L, L)) for h in range(hpg)], axis=1)
    m = jnp.concatenate([cb] * hpg, axis=1) * jnp.exp(jnp.where(causal, seg, -jnp.inf)) * dt_s
    ch_head = lax.broadcasted_iota(jnp.int32, (L, gw), 1) // P
    x_bd = jnp.concatenate([jnp.where(ch_head == h, xb, jnp.zeros_like(xb)) for h in range(hpg)], axis=0)
    y_intra = _dot(m.astype(BF16), x_bd)

    ew = jnp.concatenate([jnp.exp(acum_c), jnp.exp(acum_c[L - 1:L, :] - acum_c) * dt_c], axis=1)
    ew = _expand(ew, _onehot3(2 * hpg, 2 * gw, P))
    e_ch = ew[:, :gw]
    w_ch = ew[:, gw:]
    y = y_intra + _dot(cq, st.astype(BF16)) * e_ch + x * d_rep
    upd = _dot(b.astype(F32).T.astype(BF16), (x * w_ch).astype(BF16))
    st = st * e_ch[L - 1:L, :] + upd

    yg = y * (z * jax.nn.sigmoid(z))
    ms = jnp.mean(yg * yg, axis=-1, keepdims=True)
    return yg * lax.rsqrt(ms + RMS_EPS) * nw, st


def _ssd_scan_kernel(x_ref, b_ref, c_ref, z_ref, dtc_ref, dtr_ref, pc_ref, pr_ref, drep_ref, nw_ref,
                     o_ref, st_ref):
    n_state = b_ref.shape[1] // st_ref.shape[0]
    gw = x_ref.shape[1] // st_ref.shape[0]

    @pl.when(pl.program_id(2) == 0)
    def _():
        st_ref[...] = jnp.zeros_like(st_ref)

    for g in range(st_ref.shape[0]):
        ch = slice(g * gw, (g + 1) * gw)
        ns = slice(g * n_state, (g + 1) * n_state)
        y, st = _ssd_group(x_ref[:, ch], b_ref[:, ns], c_ref[:, ns], z_ref[:, ch], dtc_ref[0, g], dtr_ref[0, g],
                           pc_ref[g, 0:1, :], pc_ref[g, 1:2, :], pr_ref[g, :, 0:1], pr_ref[g, :, 1:2],
                           drep_ref[:, ch], nw_ref[:, ch], st_ref[g])
        st_ref[g] = st
        o_ref[:, ch] = y.astype(o_ref.dtype)


def ssd_scan(zx, xbc, dt_raw, dt_bias, a_log, d_skip, norm_w, batch, seq, gps):
    L = SSD_CHUNK
    G = SSD_GROUPS
    N = SSD_STATE
    heads = dt_raw.shape[1]
    hpg = heads // G
    gw = hpg * SSD_HEAD_DIM
    d_inner = heads * SSD_HEAD_DIM
    nc = seq // L
    n_chunks = batch * nc
    dt4 = dt_raw.reshape(n_chunks, L, G, hpg)
    dt_col = dt4.transpose(0, 2, 1, 3)
    dt_row = dt4.transpose(0, 2, 3, 1)
    par = jnp.stack([dt_bias, a_log], axis=0).astype(F32).reshape(2, G, hpg)
    par_col = par.transpose(1, 0, 2)
    par_row = par.transpose(1, 2, 0)
    d_rep = jnp.repeat(d_skip.astype(F32), SSD_HEAD_DIM).reshape(1, d_inner)
    b_off = d_inner // (gps * N)
    return pl.pallas_call(
        _ssd_scan_kernel,
        grid=(batch, G // gps, nc),
        in_specs=[pl.BlockSpec((L, gps * gw), lambda b, g, c: (b * nc + c, g)),
                  pl.BlockSpec((L, gps * N), lambda b, g, c: (b * nc + c, b_off + g)),
                  pl.BlockSpec((L, gps * N), lambda b, g, c: (b * nc + c, b_off + G // gps + g)),
                  pl.BlockSpec((L, gps * gw), lambda b, g, c: (b * nc + c, g)),
                  pl.BlockSpec((1, gps, L, hpg), lambda b, g, c: (b * nc + c, g, 0, 0)),
                  pl.BlockSpec((1, gps, hpg, L), lambda b, g, c: (b * nc + c, g, 0, 0)),
                  pl.BlockSpec((gps, 2, hpg), lambda b, g, c: (g, 0, 0)),
                  pl.BlockSpec((gps, hpg, 2), lambda b, g, c: (g, 0, 0)),
                  pl.BlockSpec((1, gps * gw), lambda b, g, c: (0, g)),
                  pl.BlockSpec((1, gps * gw), lambda b, g, c: (0, g))],
        out_specs=pl.BlockSpec((L, gps * gw), lambda b, g, c: (b * nc + c, g)),
        out_shape=jax.ShapeDtypeStruct((batch * seq, d_inner), BF16),
        scratch_shapes=[pltpu.VMEM((gps, N, gw), F32)],
        compiler_params=_params("parallel", "parallel", "arbitrary"),
        name="ssd_scan",
    )(xbc, xbc, xbc, zx, dt_col, dt_row, par_col, par_row, d_rep, norm_w.reshape(1, d_inner))


def _top16(s):
    r = s.shape[0]
    rows = lax.broadcasted_iota(jnp.int32, s.shape, 0).astype(F32)
    krow = lax.broadcasted_iota(jnp.int32, (PEER_TOPK, s.shape[1]), 0)
    rank = jnp.full(s.shape, PEER_TOPK, jnp.int32)
    vals = jnp.zeros((PEER_TOPK, s.shape[1]), F32)
    for i in range(PEER_TOPK):
        m = jnp.max(s, axis=0, keepdims=True)
        idx = jnp.min(jnp.where(s == m, rows, float(r)), axis=0, keepdims=True)
        hit = rows == idx
        rank = jnp.where(hit, i, rank)
        s = jnp.where(hit, -jnp.inf, s)
        vals = jnp.where(krow == i, m, vals)
    return vals, rank, None


def _top16_distinct(s):
    krow = lax.broadcasted_iota(jnp.int32, (PEER_TOPK, s.shape[1]), 0)
    rank = jnp.full(s.shape, PEER_TOPK, jnp.int32)
    vals = jnp.zeros((PEER_TOPK, s.shape[1]), F32)
    for i in range(PEER_TOPK):
        m = jnp.max(s, axis=0, keepdims=True)
        hit = s == m
        rank = jnp.where(hit, i, rank)
        s = jnp.where(hit, -jnp.inf, s)
        vals = jnp.where(krow == i, m, vals)
    taken = jnp.sum(jnp.where(rank < PEER_TOPK, 1.0, 0.0), axis=0, keepdims=True)
    return vals, rank, jnp.where(taken == PEER_TOPK, 1.0, 0.0)


def _peer_route_kernel(x_ref, nw_ref, wq_ref, keys_ref, xnt_ref, r2_ref, e2_ref, n1_ref, f1_ref):
    x = x_ref[...]
    ms = jnp.mean(x * x, axis=-1, keepdims=True)
    xn = x * lax.rsqrt(ms + RMS_EPS) * nw_ref[...]
    xnt_ref[...] = xn.T.astype(BF16)
    xn = xn.astype(BF16)
    q = _dot(xn, wq_ref[...]).astype(BF16)

    ok = _peer_route_heads(q, keys_ref, r2_ref, e2_ref, n1_ref, f1_ref, _top16_distinct)

    @pl.when(jnp.min(ok) < 1.0)
    def _():
        _peer_route_heads(q, keys_ref, r2_ref, e2_ref, n1_ref, f1_ref, _top16)


def _peer_route_heads(q, keys_ref, r2_ref, e2_ref, n1_ref, f1_ref, top16):
    half = keys_ref.shape[-1]
    j8 = lax.broadcasted_iota(jnp.int32, (8, q.shape[0]), 0)
    ok_all = jnp.ones((1, q.shape[0]), F32)
    for h in range(PEER_HEADS):
        s1 = _dot_nt(keys_ref[h, 0], q[:, (2 * h) * half:(2 * h + 1) * half])
        s2 = _dot_nt(keys_ref[h, 1], q[:, (2 * h + 1) * half:(2 * h + 2) * half])
        a, r1, ok1 = top16(s1)
        b, r2, ok2 = top16(s2)
        blocks = [a[0:1] + b]
        for i in range(1, 8):
            blk = a[i:i + 1] + b[0:8]
            if PEER_ROW_LEN[i] < 8:
                blk = jnp.where(j8 < PEER_ROW_LEN[i], blk, -jnp.inf)
            blocks.append(blk)
        blocks.append(a[8:16] + b[0:1])
        cand = jnp.concatenate(blocks, axis=0)
        _, rc, ok3 = top16(cand)
        for ok in (ok1, ok2, ok3):
            if ok is not None:
                ok_all = jnp.minimum(ok_all, ok)
        sel = rc < PEER_TOPK
        cmax = a[0:1] + b[0:1]
        z = jnp.sum(jnp.where(sel, jnp.exp(cand - cmax), 0.0), axis=0, keepdims=True)
        self32 = jnp.where(sel, 1.0, 0.0)
        n1 = jnp.zeros(s1.shape, F32)
        off = 0
        for i in range(8):
            ln = 16 if i == 0 else 8
            cnt = jnp.sum(self32[off:off + ln], axis=0, keepdims=True)
            n1 = jnp.where(r1 == i, cnt, n1)
            off += ln
        for i in range(8, 16):
            n1 = jnp.where(r1 == i, self32[off + i - 8:off + i - 7], n1)
        r2_ref[h] = r2.astype(F32).astype(r2_ref.dtype)
        n1_ref[h] = n1
        e2_ref[h] = jnp.where(r2 < PEER_TOPK, jnp.exp(s2 - b[0:1]), 0.0).astype(e2_ref.dtype)
        f1_ref[h] = jnp.where(r1 < PEER_TOPK, jnp.exp(s1 - a[0:1]), 0.0) / z
    return ok_all


def peer_route(x, nw, w_q, sub_keys, tm):
    t, d = x.shape
    nk = sub_keys.shape[2]
    fac = jax.ShapeDtypeStruct((PEER_HEADS, nk, t), F32)
    fac16 = jax.ShapeDtypeStruct((PEER_HEADS, nk, t), BF16)
    fspec = pl.BlockSpec((PEER_HEADS, nk, tm), lambda i: (0, 0, i))
    return pl.pallas_call(
        _peer_route_kernel,
        grid=(t // tm,),
        in_specs=[pl.BlockSpec((tm, d), lambda i: (i, 0)),
                  pl.BlockSpec((1, d), lambda i: (0, 0)),
                  pl.BlockSpec(w_q.shape, lambda i: (0, 0)),
                  pl.BlockSpec(sub_keys.shape, lambda i: (0, 0, 0, 0))],
        out_specs=[pl.BlockSpec((d, tm), lambda i: (0, i)), fspec, fspec, fspec, fspec],
        out_shape=[jax.ShapeDtypeStruct((d, t), BF16), fac16, fac16, fac, fac],
        compiler_params=_params("parallel"),
        name="peer_route",
    )(x, nw.reshape(1, d), w_q, sub_keys)


def _gelu(h):
    return 0.5 * h * (1.0 + lax.erf(h * (1.0 / math.sqrt(2.0))))


def _peer_experts_kernel(xnt_ref, u_ref, vt_ref, r2_ref, e2_ref, n1_ref, f1_ref, x_ref, o_ref, acc_ref, g_ref,
                         *, n_keys, sub):
    j = pl.program_id(1)

    @pl.when(j == 0)
    def _():
        acc_ref[...] = jnp.zeros_like(acc_ref)

    eb = u_ref.shape[0]
    m = xnt_ref.shape[1]
    rows16 = 16

    for cc in range(eb // n_keys):
        c = j * (eb // n_keys) + cc
        n1 = [jnp.broadcast_to(n1_ref[h, pl.ds(c, 1), :], (rows16, m)).astype(BF16) for h in range(PEER_HEADS)]
        f1 = [jnp.broadcast_to(f1_ref[h, pl.ds(c, 1), :], (rows16, m)).astype(BF16) for h in range(PEER_HEADS)]
        for k in range(n_keys // rows16):
            rows = slice(k * rows16, (k + 1) * rows16)
            g = jnp.zeros((rows16, m), BF16)
            for h in range(PEER_HEADS):
                g = g + jnp.where(r2_ref[h, rows, :] < n1[h], e2_ref[h, rows, :], jnp.zeros_like(g)) * f1[h]
            g_ref[cc * n_keys + k * rows16:cc * n_keys + (k + 1) * rows16, :] = g

    xnt = xnt_ref[...]
    hts = [_dot(u_ref[sb * sub:(sb + 1) * sub, :], xnt) for sb in range(eb // sub)]
    for sb, ht in enumerate(hts):
        act = g_ref[sb * sub:(sb + 1) * sub, :] * _gelu(ht).astype(BF16)
        acc_ref[...] = _dot(vt_ref[:, sb * sub:(sb + 1) * sub], act) + acc_ref[...]

    @pl.when(j == pl.num_programs(1) - 1)
    def _():
        o_ref[...] = x_ref[...] + acc_ref[...].T


def peer_experts(x, xnt, u, vt, r2, e2, n1, f1, tm, eb, sub):
    t, d = x.shape
    n_exp = u.shape[0]
    n_keys = r2.shape[1]
    fspec = pl.BlockSpec((PEER_HEADS, n_keys, tm), lambda i, j: (0, 0, i))
    return pl.pallas_call(
        functools.partial(_peer_experts_kernel, n_keys=n_keys, sub=sub),
        grid=(t // tm, n_exp // eb),
        in_specs=[pl.BlockSpec((d, tm), lambda i, j: (0, i)),
                  pl.BlockSpec((eb, d), lambda i, j: (j, 0)),
                  pl.BlockSpec((d, eb), lambda i, j: (0, j)),
                  fspec, fspec, fspec, fspec,
                  pl.BlockSpec((tm, d), lambda i, j: (i, 0))],
        out_specs=pl.BlockSpec((tm, d), lambda i, j: (i, 0)),
        out_shape=jax.ShapeDtypeStruct((t, d), F32),
        scratch_shapes=[pltpu.VMEM((d, tm), F32), pltpu.VMEM((eb, tm), BF16)],
        compiler_params=_params("parallel", "arbitrary"),
        name="peer_experts",
    )(xnt, u, vt, r2, e2, n1, f1, x)


def peer_ffn(x, nw, w_q, sub_keys, u, v, route_tm, tm, eb, sub):
    xnt, r2, e2, n1, f1 = peer_route(x, nw, w_q.astype(BF16), sub_keys.astype(BF16), route_tm)
    return peer_experts(x, xnt, u.astype(BF16), v.T.astype(BF16), r2, e2, n1, f1, tm, eb, sub)


def _rmsnorm_kernel(x_ref, nw_ref, o_ref):
    x = x_ref[...]
    ms = jnp.mean(x * x, axis=-1, keepdims=True)
    o_ref[...] = x * lax.rsqrt(ms + RMS_EPS) * nw_ref[...]


def rmsnorm(x, nw, tm):
    t, d = x.shape
    return pl.pallas_call(
        _rmsnorm_kernel,
        grid=(t // tm,),
        in_specs=[pl.BlockSpec((tm, d), lambda i: (i, 0)), pl.BlockSpec((1, d), lambda i: (0, 0))],
        out_specs=pl.BlockSpec((tm, d), lambda i: (i, 0)),
        out_shape=jax.ShapeDtypeStruct((t, d), F32),
        compiler_params=_params("parallel"),
        name="rmsnorm",
    )(x, nw.reshape(1, d))


def _tile(n, pref):
    t = min(n, pref)
    while n % t:
        t -= LANES
    return t


def even_layer(x, nw, w_in, conv_w, w_out, batch, seq):
    t = x.shape[0]
    conv_ch = conv_w.shape[1]
    w_in = w_in.astype(BF16)
    tm = _tile(t, 1024)
    n_a = 3 * conv_ch
    n_b = w_in.shape[1] - n_a
    proj_a = norm_matmul(x, nw, w_in, 0, n_a, F32, tm, _tile(n_a, 1024))
    qkv = norm_matmul(x, nw, w_in, n_a, n_b, BF16, tm, _tile(math.gcd(n_a, n_b), 1024))
    y_conv = even_conv(proj_a, conv_w, batch, seq, _tile(conv_ch, 256))
    y_sb = sb_attention(qkv, batch, seq, _tile(seq, 256), 4)
    return matmul_resid([y_conv, y_sb], w_out.astype(BF16), x, tm, _tile(x.shape[1], 1024))


def ssd_layer(x, nw, w_in, conv_w, conv_b, dt_bias, a_log, d_skip, norm_w, w_out, batch, seq):
    t = x.shape[0]
    heads = dt_bias.shape[0]
    d_inner = heads * SSD_HEAD_DIM
    conv_dim = conv_w.shape[1]
    w_in = w_in.astype(BF16)
    tm = _tile(t, 1024)
    zx = norm_matmul(x, nw, w_in, 0, d_inner + conv_dim, F32, tm, _tile(d_inner + conv_dim, 1024))
    w_dt = jnp.pad(w_in[:, d_inner + conv_dim:], ((0, 0), (0, LANES - heads)))
    dt_raw = norm_matmul(x, nw, w_dt, 0, LANES, F32, tm, LANES)[:, :heads]
    xbc = ssd_conv(zx, conv_w, conv_b, batch, seq, d_inner, _tile(conv_dim, 512))
    y = ssd_scan(zx, xbc, dt_raw, dt_bias, a_log, d_skip, norm_w, batch, seq, 8)
    return matmul_resid([y], w_out.astype(BF16), x, tm, _tile(x.shape[1], 512))


def kernel(x, norm_mix, norm_ffn, norm_final, ev_w_in, ev_conv_w, ev_w_out, ssd_w_in, ssd_conv_w, ssd_conv_b, ssd_dt_bias, ssd_a_log, ssd_d, ssd_norm, ssd_w_out, peer_w_q, peer_sub_keys, peer_u, peer_v):
    batch, seq, d = x.shape
    t = batch * seq
    x = x.reshape(t, d)
    depth = norm_mix.shape[0]
    for i in range(depth):
        j = i // 2
        if i % 2 == 0:
            x = even_layer(x, norm_mix[i], ev_w_in[j], ev_conv_w[j], ev_w_out[j], batch, seq)
        else:
            x = ssd_layer(x, norm_mix[i], ssd_w_in[j], ssd_conv_w[j], ssd_conv_b[j], ssd_dt_bias[j],
                          ssd_a_log[j], ssd_d[j], ssd_norm[j], ssd_w_out[j], batch, seq)
        x = peer_ffn(x, norm_ffn[i], peer_w_q[i], peer_sub_keys[i], peer_u[i], peer_v[i],
                     _tile(t, 128), _tile(t, 512), 1024, 256)
    return rmsnorm(x, norm_final, _tile(t, 512)).reshape(batch, seq, d)
```

```python
import functools
import math

import jax
import jax.numpy as jnp
from jax import lax
from jax.experimental import pallas as pl
from jax.experimental.pallas import tpu as pltpu

F32 = jnp.float32
BF16 = jnp.bfloat16

RMS_EPS = 1e-6
LANES = 128
VMEM_LIMIT = 60 * 1024 * 1024

SB_HEADS = 8
SB_HEAD_DIM = 128
SB_DROPPED = -1e30
SSD_HEAD_DIM = 64
SSD_GROUPS = 8
SSD_STATE = 128
SSD_CHUNK = 128
PEER_HEADS = 8
PEER_KEYS = 128
PEER_TOPK = 16
PEER_ROW_LEN = (16, 8, 5, 4, 3, 2, 2, 2)


def _params(*sem):
    return pltpu.CompilerParams(dimension_semantics=sem, vmem_limit_bytes=VMEM_LIMIT)


def _split3(v):
    hi = v.astype(BF16)
    r = v - hi.astype(F32)
    mid = r.astype(BF16)
    lo = (r - mid.astype(F32)).astype(BF16)
    return hi, mid, lo


def _dot(a, b):
    return jnp.dot(a, b, preferred_element_type=F32)


def _dot_nt(a, b):
    return lax.dot_general(a, b, (((1,), (1,)), ((), ())), preferred_element_type=F32)


def _norm_matmul_kernel(x_ref, nw_ref, w_ref, o_ref, xn_ref):
    @pl.when(pl.program_id(1) == 0)
    def _():
        x = x_ref[...]
        ms = jnp.mean(x * x, axis=-1, keepdims=True)
        xn_ref[...] = (x * lax.rsqrt(ms + RMS_EPS) * nw_ref[...]).astype(BF16)

    o_ref[...] = _dot(xn_ref[...], w_ref[...]).astype(o_ref.dtype)


def norm_matmul(x, nw, w, col0, n, out_dtype, tm, tn):
    m, k = x.shape
    assert col0 % tn == 0 and n % tn == 0 and col0 + n <= w.shape[1]
    cb = col0 // tn
    return pl.pallas_call(
        _norm_matmul_kernel,
        grid=(m // tm, n // tn),
        in_specs=[pl.BlockSpec((tm, k), lambda i, j: (i, 0)),
                  pl.BlockSpec((1, k), lambda i, j: (0, 0)),
                  pl.BlockSpec((k, tn), lambda i, j: (0, cb + j))],
        out_specs=pl.BlockSpec((tm, tn), lambda i, j: (i, j)),
        out_shape=jax.ShapeDtypeStruct((m, n), out_dtype),
        scratch_shapes=[pltpu.VMEM((tm, k), BF16)],
        compiler_params=_params("parallel", "arbitrary"),
        name="norm_matmul",
    )(x, nw.reshape(1, k), w)


def _matmul_resid_kernel(*refs, n_in):
    a_refs = refs[:n_in]
    w_refs = refs[n_in:2 * n_in]
    r_ref = refs[2 * n_in]
    o_ref = refs[2 * n_in + 1]
    acc = r_ref[...]
    for a_ref, w_ref in zip(a_refs, w_refs):
        acc = acc + _dot(a_ref[...], w_ref[...])
    o_ref[...] = acc


def matmul_resid(a_list, w, resid, tm, tn):
    m, n = resid.shape
    n_in = len(a_list)
    ka = a_list[0].shape[1]
    assert all(a.shape[1] == ka for a in a_list) and ka * n_in == w.shape[0]
    in_specs = ([pl.BlockSpec((tm, ka), lambda i, j: (i, 0)) for _ in a_list]
                + [pl.BlockSpec((ka, tn), functools.partial(lambda i, j, r: (r, j), r=r)) for r in range(n_in)]
                + [pl.BlockSpec((tm, tn), lambda i, j: (i, j))])
    return pl.pallas_call(
        functools.partial(_matmul_resid_kernel, n_in=n_in),
        grid=(m // tm, n // tn),
        in_specs=in_specs,
        out_specs=pl.BlockSpec((tm, tn), lambda i, j: (i, j)),
        out_shape=jax.ShapeDtypeStruct((m, n), F32),
        compiler_params=_params("parallel", "arbitrary"),
        name="matmul_resid",
    )(*a_list, *([w] * n_in), resid)


def _causal_conv(u, w_ref):
    k_w = w_ref.shape[0]
    sub = 8
    row = lax.broadcasted_iota(jnp.int32, (sub, u.shape[1]), 0)
    out = u * w_ref[k_w - 1:k_w, :]
    for k in range(k_w - 1):
        shift = k_w - 1 - k
        rolled = pltpu.roll(u, shift, 0)
        top = jnp.where(row >= shift, rolled[:sub], 0.0)
        shifted = jnp.concatenate([top, rolled[sub:]], axis=0)
        out = out + shifted * w_ref[k:k + 1, :]
    return out


def _even_conv_kernel(gb_ref, gc_ref, xv_ref, w_ref, o_ref):
    u = gc_ref[...] * xv_ref[...]
    o_ref[...] = (gb_ref[...] * _causal_conv(u, w_ref)).astype(o_ref.dtype)


def even_conv(proj, conv_w, batch, seq, cb):
    c = conv_w.shape[1]
    nb = c // cb
    return pl.pallas_call(
        _even_conv_kernel,
        grid=(batch, nb),
        in_specs=[pl.BlockSpec((seq, cb), lambda b, j: (b, j)),
                  pl.BlockSpec((seq, cb), lambda b, j: (b, nb + j)),
                  pl.BlockSpec((seq, cb), lambda b, j: (b, 2 * nb + j)),
                  pl.BlockSpec((conv_w.shape[0], cb), lambda b, j: (0, j))],
        out_specs=pl.BlockSpec((seq, cb), lambda b, j: (b, j)),
        out_shape=jax.ShapeDtypeStruct((batch * seq, c), BF16),
        compiler_params=_params("parallel", "parallel"),
        name="even_conv",
    )(proj, proj, proj, conv_w)


def _ssd_conv_kernel(u_ref, w_ref, b_ref, o_ref):
    y = _causal_conv(u_ref[...], w_ref) + b_ref[...]
    o_ref[...] = (y * jax.nn.sigmoid(y)).astype(o_ref.dtype)


def ssd_conv(zx, conv_w, conv_b, batch, seq, col0, cb):
    c = conv_w.shape[1]
    off = col0 // cb
    return pl.pallas_call(
        _ssd_conv_kernel,
        grid=(batch, c // cb),
        in_specs=[pl.BlockSpec((seq, cb), lambda b, j: (b, off + j)),
                  pl.BlockSpec((conv_w.shape[0], cb), lambda b, j: (0, j)),
                  pl.BlockSpec((1, cb), lambda b, j: (0, j))],
        out_specs=pl.BlockSpec((seq, cb), lambda b, j: (b, j)),
        out_shape=jax.ShapeDtypeStruct((batch * seq, c), BF16),
        compiler_params=_params("parallel", "parallel"),
        name="ssd_conv",
    )(zx, conv_w, conv_b.reshape(1, c))


def _sb_attn_kernel(q_ref, k_ref, v_ref, o_ref, z_ref, a_ref, *, blk, scale):
    qi = pl.program_id(2)
    nq = pl.num_programs(2)
    d = SB_HEAD_DIM
    heads = range(q_ref.shape[1] // d)
    w = LANES
    nsub = blk // w
    jj = lax.broadcasted_iota(jnp.int32, (w, 2 * w), 0)
    ss = lax.broadcasted_iota(jnp.int32, (w, 2 * w), 1)
    cum_mat = jnp.where((ss >= w) | (jj > ss), 1.0, 0.0).astype(BF16)
    row = lax.broadcasted_iota(jnp.int32, (blk, blk), 0)
    col = lax.broadcasted_iota(jnp.int32, (blk, blk), 1)

    def rows_of(kb):
        return pl.ds(pl.multiple_of(jnp.clip(kb, 0, nq - 1) * blk, blk), blk)

    def scores(kb, h, slot):
        hd = slice(h * d, (h + 1) * d)
        z_ref[2 * h + slot] = _dot_nt(q_ref[:, hd], k_ref[rows_of(kb), hd]) * scale

    def weights(kb, h, slot, carry):
        limit = jnp.where(kb == qi, row, jnp.where(kb < 0, 0, blk))
        z = jnp.where(col < limit, z_ref[2 * h + slot], SB_DROPPED)
        log_beta = jnp.minimum(z, 0.0) - jnp.log(1.0 + jnp.exp(-jnp.abs(z)))
        ls = log_beta - z
        hi = ls.astype(BF16)
        lo = (ls - hi.astype(F32)).astype(BF16)
        hi = jnp.concatenate([hi[:, i * w:(i + 1) * w] for i in range(nsub)], axis=0)
        lo = jnp.concatenate([lo[:, i * w:(i + 1) * w] for i in range(nsub)], axis=0)
        cs = _dot(hi, cum_mat) + _dot(lo, cum_mat)
        after = [None] * nsub
        for i in reversed(range(nsub)):
            after[i] = cs[i * blk:(i + 1) * blk, :w] + carry
            carry = carry + cs[i * blk:(i + 1) * blk, w:]
        a_ref[2 * h + slot] = jnp.exp(log_beta + jnp.concatenate(after, axis=1)).astype(BF16)
        return carry

    def values(kb, h, slot, acc):
        return acc + _dot(a_ref[2 * h + slot], v_ref[rows_of(kb), h * d:(h + 1) * d])

    a_ref[...] = jnp.zeros_like(a_ref)
    for h in heads:
        scores(qi, h, 0)
        scores(qi - 1, h, 1)

    def body(it, state):
        carry, acc = state
        ka = qi - 2 * it
        acc = tuple(values(ka + 1, h, 1, values(ka + 2, h, 0, acc[h])) for h in heads)
        carry = tuple(weights(ka - 1, h, 1, weights(ka, h, 0, carry[h])) for h in heads)
        for h in heads:
            scores(ka - 2, h, 0)
            scores(ka - 3, h, 1)
        return carry, acc

    n_iter = (qi + 2) // 2
    carry = tuple(jnp.zeros((blk, w), F32) for _ in heads)
    acc = tuple(jnp.zeros((blk, d), F32) for _ in heads)
    carry, acc = lax.fori_loop(0, n_iter, body, (carry, acc))
    last = qi - 2 * (n_iter - 1)
    for h in heads:
        o_ref[:, h * d:(h + 1) * d] = values(last - 1, h, 1, values(last, h, 0, acc[h])).astype(o_ref.dtype)


def sb_attention(qkv, batch, seq, blk, hps):
    d = SB_HEAD_DIM
    nq = seq // blk
    hg = SB_HEADS // hps
    return pl.pallas_call(
        functools.partial(_sb_attn_kernel, blk=blk, scale=d ** -0.5),
        grid=(batch, hg, nq),
        in_specs=[pl.BlockSpec((blk, hps * d), lambda b, h, i: (b * nq + i, h)),
                  pl.BlockSpec((seq, hps * d), lambda b, h, i: (b, hg + h)),
                  pl.BlockSpec((seq, hps * d), lambda b, h, i: (b, 2 * hg + h))],
        out_specs=pl.BlockSpec((blk, hps * d), lambda b, h, i: (b * nq + i, h)),
        out_shape=jax.ShapeDtypeStruct((batch * seq, SB_HEADS * d), BF16),
        scratch_shapes=[pltpu.VMEM((2 * hps, blk, blk), F32), pltpu.VMEM((2 * hps, blk, blk), BF16)],
        compiler_params=_params("parallel", "parallel", "arbitrary"),
        name="sb_attention",
    )(qkv, qkv, qkv)


def _softplus(x):
    return jnp.maximum(x, 0.0) + jnp.log1p(jnp.exp(-jnp.abs(x)))


def _onehot3(k, n, width):
    rr = lax.broadcasted_iota(jnp.int32, (3 * k, n), 0) % k
    cc = lax.broadcasted_iota(jnp.int32, (3 * k, n), 1) // width
    return jnp.where(rr == cc, 1.0, 0.0).astype(BF16)


def _expand(v, mat3):
    return _dot(jnp.concatenate(_split3(v), axis=1), mat3)


def _ssd_group(x, b, c, z, dt_c, dt_r, bias_c, alog_c, bias_r, alog_r, d_rep, nw, st):
    L, gw = x.shape
    xb = x
    x = x.astype(F32)
    hpg = dt_c.shape[-1]
    P = SSD_HEAD_DIM
    dt_c = _softplus(dt_c + bias_c)
    dt_r = _softplus(dt_r + bias_r)
    ti = lax.broadcasted_iota(jnp.int32, (L, L), 0)
    si = lax.broadcasted_iota(jnp.int32, (L, L), 1)
    tri = jnp.where(si <= ti, 1.0, 0.0).astype(BF16)
    tri_t3 = jnp.concatenate([jnp.where(ti <= si, 1.0, 0.0).astype(BF16)] * 3, axis=0)
    h0, h1, h2 = _split3(dt_c * -jnp.exp(alog_c))
    acum_c = _dot(tri, h0) + _dot(tri, h1) + _dot(tri, h2)
    acum_r = _expand(dt_r * -jnp.exp(alog_r), tri_t3)

    bq = b.astype(BF16)
    cq = c.astype(BF16)
    cb = _dot_nt(cq, bq)
    seg = _expand(acum_c, _onehot3(hpg, hpg * L, L)) - jnp.concatenate(
        [jnp.broadcast_to(acum_r[h:h + 1, :], (L, L)) for h in range(hpg)], axis=1)
    causal = jnp.concatenate([si <= ti] * hpg, axis=1)
    dt_s = jnp.concatenate([jnp.broadcast_to(dt_r[h:h + 1, :], (L, L)) for h in range(hpg)], axis=1)
    m = jnp.concatenate([cb] * hpg, axis=1) * jnp.exp(jnp.where(causal, seg, -jnp.inf)) * dt_s
    ch_head = lax.broadcasted_iota(jnp.int32, (L, gw), 1) // P
    x_bd = jnp.concatenate([jnp.where(ch_head == h, xb, jnp.zeros_like(xb)) for h in range(hpg)], axis=0)
    y_intra = _dot(m.astype(BF16), x_bd)

    ew = jnp.concatenate([jnp.exp(acum_c), jnp.exp(acum_c[L - 1:L, :] - acum_c) * dt_c], axis=1)
    ew = _expand(ew, _onehot3(2 * hpg, 2 * gw, P))
    e_ch = ew[:, :gw]
    w_ch = ew[:, gw:]
    y = y_intra + _dot(cq, st.astype(BF16)) * e_ch + x * d_rep
    upd = _dot(b.astype(F32).T.astype(BF16), (x * w_ch).astype(BF16))
    st = st * e_ch[L - 1:L, :] + upd

    yg = y * (z * jax.nn.sigmoid(z))
    ms = jnp.mean(yg * yg, axis=-1, keepdims=True)
    return yg * lax.rsqrt(ms + RMS_EPS) * nw, st


def _ssd_scan_kernel(x_ref, b_ref, c_ref, z_ref, dtc_ref, dtr_ref, pc_ref, pr_ref, drep_ref, nw_ref,
                     o_ref, st_ref):
    n_state = b_ref.shape[1] // st_ref.shape[0]
    gw = x_ref.shape[1] // st_ref.shape[0]

    @pl.when(pl.program_id(2) == 0)
    def _():
        st_ref[...] = jnp.zeros_like(st_ref)

    for g in range(st_ref.shape[0]):
        ch = slice(g * gw, (g + 1) * gw)
        ns = slice(g * n_state, (g + 1) * n_state)
        y, st = _ssd_group(x_ref[:, ch], b_ref[:, ns], c_ref[:, ns], z_ref[:, ch], dtc_ref[0, g], dtr_ref[0, g],
                           pc_ref[g, 0:1, :], pc_ref[g, 1:2, :], pr_ref[g, :, 0:1], pr_ref[g, :, 1:2],
                           drep_ref[:, ch], nw_ref[:, ch], st_ref[g])
        st_ref[g] = st
        o_ref[:, ch] = y.astype(o_ref.dtype)


def ssd_scan(zx, xbc, dt_raw, dt_bias, a_log, d_skip, norm_w, batch, seq, gps):
    L = SSD_CHUNK
    G = SSD_GROUPS
    N = SSD_STATE
    heads = dt_raw.shape[1]
    hpg = heads // G
    gw = hpg * SSD_HEAD_DIM
    d_inner = heads * SSD_HEAD_DIM
    nc = seq // L
    n_chunks = batch * nc
    dt4 = dt_raw.reshape(n_chunks, L, G, hpg)
    dt_col = dt4.transpose(0, 2, 1, 3)
    dt_row = dt4.transpose(0, 2, 3, 1)
    par = jnp.stack([dt_bias, a_log], axis=0).astype(F32).reshape(2, G, hpg)
    par_col = par.transpose(1, 0, 2)
    par_row = par.transpose(1, 2, 0)
    d_rep = jnp.repeat(d_skip.astype(F32), SSD_HEAD_DIM).reshape(1, d_inner)
    b_off = d_inner // (gps * N)
    return pl.pallas_call(
        _ssd_scan_kernel,
        grid=(batch, G // gps, nc),
        in_specs=[pl.BlockSpec((L, gps * gw), lambda b, g, c: (b * nc + c, g)),
                  pl.BlockSpec((L, gps * N), lambda b, g, c: (b * nc + c, b_off + g)),
                  pl.BlockSpec((L, gps * N), lambda b, g, c: (b * nc + c, b_off + G // gps + g)),
                  pl.BlockSpec((L, gps * gw), lambda b, g, c: (b * nc + c, g)),
                  pl.BlockSpec((1, gps, L, hpg), lambda b, g, c: (b * nc + c, g, 0, 0)),
                  pl.BlockSpec((1, gps, hpg, L), lambda b, g, c: (b * nc + c, g, 0, 0)),
                  pl.BlockSpec((gps, 2, hpg), lambda b, g, c: (g, 0, 0)),
                  pl.BlockSpec((gps, hpg, 2), lambda b, g, c: (g, 0, 0)),
                  pl.BlockSpec((1, gps * gw), lambda b, g, c: (0, g)),
                  pl.BlockSpec((1, gps * gw), lambda b, g, c: (0, g))],
        out_specs=pl.BlockSpec((L, gps * gw), lambda b, g, c: (b * nc + c, g)),
        out_shape=jax.ShapeDtypeStruct((batch * seq, d_inner), BF16),
        scratch_shapes=[pltpu.VMEM((gps, N, gw), F32)],
        compiler_params=_params("parallel", "parallel", "arbitrary"),
        name="ssd_scan",
    )(xbc, xbc, xbc, zx, dt_col, dt_row, par_col, par_row, d_rep, norm_w.reshape(1, d_inner))


def _top16(s):
    r = s.shape[0]
    rows = lax.broadcasted_iota(jnp.int32, s.shape, 0).astype(F32)
    krow = lax.broadcasted_iota(jnp.int32, (PEER_TOPK, s.shape[1]), 0)
    rank = jnp.full(s.shape, PEER_TOPK, jnp.int32)
    vals = jnp.zeros((PEER_TOPK, s.shape[1]), F32)
    for i in range(PEER_TOPK):
        m = jnp.max(s, axis=0, keepdims=True)
        idx = jnp.min(jnp.where(s == m, rows, float(r)), axis=0, keepdims=True)
        hit = rows == idx
        rank = jnp.where(hit, i, rank)
        s = jnp.where(hit, -jnp.inf, s)
        vals = jnp.where(krow == i, m, vals)
    return vals, rank, None


def _top16_distinct(s):
    krow = lax.broadcasted_iota(jnp.int32, (PEER_TOPK, s.shape[1]), 0)
    rank = jnp.full(s.shape, PEER_TOPK, jnp.int32)
    vals = jnp.zeros((PEER_TOPK, s.shape[1]), F32)
    for i in range(PEER_TOPK):
        m = jnp.max(s, axis=0, keepdims=True)
        hit = s == m
        rank = jnp.where(hit, i, rank)
        s = jnp.where(hit, -jnp.inf, s)
        vals = jnp.where(krow == i, m, vals)
    taken = jnp.sum(jnp.where(rank < PEER_TOPK, 1.0, 0.0), axis=0, keepdims=True)
    return vals, rank, jnp.where(taken == PEER_TOPK, 1.0, 0.0)


def _peer_route_kernel(x_ref, nw_ref, wq_ref, keys_ref, xnt_ref, r2_ref, e2_ref, n1_ref, f1_ref):
    x = x_ref[...]
    ms = jnp.mean(x * x, axis=-1, keepdims=True)
    xn = x * lax.rsqrt(ms + RMS_EPS) * nw_ref[...]
    xnt_ref[...] = xn.T.astype(BF16)
    xn = xn.astype(BF16)
    q = _dot(xn, wq_ref[...]).astype(BF16)

    ok = _peer_route_heads(q, keys_ref, r2_ref, e2_ref, n1_ref, f1_ref, _top16_distinct)

    @pl.when(jnp.min(ok) < 1.0)
    def _():
        _peer_route_heads(q, keys_ref, r2_ref, e2_ref, n1_ref, f1_ref, _top16)


def _peer_route_heads(q, keys_ref, r2_ref, e2_ref, n1_ref, f1_ref, top16):
    half = keys_ref.shape[-1]
    j8 = lax.broadcasted_iota(jnp.int32, (8, q.shape[0]), 0)
    ok_all = jnp.ones((1, q.shape[0]), F32)
    for h in range(PEER_HEADS):
        s1 = _dot_nt(keys_ref[h, 0], q[:, (2 * h) * half:(2 * h + 1) * half])
        s2 = _dot_nt(keys_ref[h, 1], q[:, (2 * h + 1) * half:(2 * h + 2) * half])
        a, r1, ok1 = top16(s1)
        b, r2, ok2 = top16(s2)
        blocks = [a[0:1] + b]
        for i in range(1, 8):
            blk = a[i:i + 1] + b[0:8]
            if PEER_ROW_LEN[i] < 8:
                blk = jnp.where(j8 < PEER_ROW_LEN[i], blk, -jnp.inf)
            blocks.append(blk)
        blocks.append(a[8:16] + b[0:1])
        cand = jnp.concatenate(blocks, axis=0)
        _, rc, ok3 = top16(cand)
        for ok in (ok1, ok2, ok3):
            if ok is not None:
                ok_all = jnp.minimum(ok_all, ok)
        sel = rc < PEER_TOPK
        cmax = a[0:1] + b[0:1]
        z = jnp.sum(jnp.where(sel, jnp.exp(cand - cmax), 0.0), axis=0, keepdims=True)
        self32 = jnp.where(sel, 1.0, 0.0)
        n1 = jnp.zeros(s1.shape, F32)
        off = 0
        for i in range(8):
            ln = 16 if i == 0 else 8
            cnt = jnp.sum(self32[off:off + ln], axis=0, keepdims=True)
            n1 = jnp.where(r1 == i, cnt, n1)
            off += ln
        for i in range(8, 16):
            n1 = jnp.where(r1 == i, self32[off + i - 8:off + i - 7], n1)
        r2_ref[h] = r2.astype(F32).astype(r2_ref.dtype)
        n1_ref[h] = n1
        e2_ref[h] = jnp.where(r2 < PEER_TOPK, jnp.exp(s2 - b[0:1]), 0.0).astype(e2_ref.dtype)
        f1_ref[h] = jnp.where(r1 < PEER_TOPK, jnp.exp(s1 - a[0:1]), 0.0) / z
    return ok_all


def peer_route(x, nw, w_q, sub_keys, tm):
    t, d = x.shape
    nk = sub_keys.shape[2]
    fac = jax.ShapeDtypeStruct((PEER_HEADS, nk, t), F32)
    fac16 = jax.ShapeDtypeStruct((PEER_HEADS, nk, t), BF16)
    fspec = pl.BlockSpec((PEER_HEADS, nk, tm), lambda i: (0, 0, i))
    return pl.pallas_call(
        _peer_route_kernel,
        grid=(t // tm,),
        in_specs=[pl.BlockSpec((tm, d), lambda i: (i, 0)),
                  pl.BlockSpec((1, d), lambda i: (0, 0)),
                  pl.BlockSpec(w_q.shape, lambda i: (0, 0)),
                  pl.BlockSpec(sub_keys.shape, lambda i: (0, 0, 0, 0))],
        out_specs=[pl.BlockSpec((d, tm), lambda i: (0, i)), fspec, fspec, fspec, fspec],
        out_shape=[jax.ShapeDtypeStruct((d, t), BF16), fac16, fac16, fac, fac],
        compiler_params=_params("parallel"),
        name="peer_route",
    )(x, nw.reshape(1, d), w_q, sub_keys)


def _gelu(h):
    return 0.5 * h * (1.0 + lax.erf(h * (1.0 / math.sqrt(2.0))))


def _peer_experts_kernel(xnt_ref, u_ref, vt_ref, r2_ref, e2_ref, n1_ref, f1_ref, x_ref, fnw_ref, o_ref,
                         acc_ref, g_ref, *, n_keys, sub, final_norm):
    j = pl.program_id(1)

    @pl.when(j == 0)
    def _():
        acc_ref[...] = jnp.zeros_like(acc_ref)

    eb = u_ref.shape[0]
    m = xnt_ref.shape[1]
    rows16 = 16

    for cc in range(eb // n_keys):
        c = j * (eb // n_keys) + cc
        n1 = [jnp.broadcast_to(n1_ref[h, pl.ds(c, 1), :], (rows16, m)).astype(BF16) for h in range(PEER_HEADS)]
        f1 = [jnp.broadcast_to(f1_ref[h, pl.ds(c, 1), :], (rows16, m)).astype(BF16) for h in range(PEER_HEADS)]
        for k in range(n_keys // rows16):
            rows = slice(k * rows16, (k + 1) * rows16)
            g = jnp.zeros((rows16, m), BF16)
            for h in range(PEER_HEADS):
                g = g + jnp.where(r2_ref[h, rows, :] < n1[h], e2_ref[h, rows, :], jnp.zeros_like(g)) * f1[h]
            g_ref[cc * n_keys + k * rows16:cc * n_keys + (k + 1) * rows16, :] = g

    xnt = xnt_ref[...]
    hts = [_dot(u_ref[sb * sub:(sb + 1) * sub, :], xnt) for sb in range(eb // sub)]
    for sb, ht in enumerate(hts):
        act = g_ref[sb * sub:(sb + 1) * sub, :] * _gelu(ht).astype(BF16)
        acc_ref[...] = _dot(vt_ref[:, sb * sub:(sb + 1) * sub], act) + acc_ref[...]

    @pl.when(j == pl.num_programs(1) - 1)
    def _():
        for r in range(0, m, LANES):
            rows = slice(r, r + LANES)
            y = x_ref[rows, :] + acc_ref[:, rows].T
            if final_norm:
                ms = jnp.mean(y * y, axis=-1, keepdims=True)
                y = y * lax.rsqrt(ms + RMS_EPS) * fnw_ref[...]
            o_ref[rows, :] = y


def peer_experts(x, xnt, u, vt, r2, e2, n1, f1, final_nw, tm, eb, sub):
    t, d = x.shape
    n_exp = u.shape[0]
    n_keys = r2.shape[1]
    fspec = pl.BlockSpec((PEER_HEADS, n_keys, tm), lambda i, j: (0, 0, i))
    return pl.pallas_call(
        functools.partial(_peer_experts_kernel, n_keys=n_keys, sub=sub, final_norm=final_nw is not None),
        grid=(t // tm, n_exp // eb),
        in_specs=[pl.BlockSpec((d, tm), lambda i, j: (0, i)),
                  pl.BlockSpec((eb, d), lambda i, j: (j, 0)),
                  pl.BlockSpec((d, eb), lambda i, j: (0, j)),
                  fspec, fspec, fspec, fspec,
                  pl.BlockSpec((tm, d), lambda i, j: (i, 0)),
                  pl.BlockSpec((1, d), lambda i, j: (0, 0))],
        out_specs=pl.BlockSpec((tm, d), lambda i, j: (i, 0)),
        out_shape=jax.ShapeDtypeStruct((t, d), F32),
        scratch_shapes=[pltpu.VMEM((d, tm), F32), pltpu.VMEM((eb, tm), BF16)],
        compiler_params=_params("parallel", "arbitrary"),
        name="peer_experts",
    )(xnt, u, vt, r2, e2, n1, f1, x, (jnp.ones((d,), F32) if final_nw is None else final_nw).reshape(1, d))


def peer_ffn(x, nw, w_q, sub_keys, u, v, final_nw, route_tm, tm, eb, sub):
    xnt, r2, e2, n1, f1 = peer_route(x, nw, w_q.astype(BF16), sub_keys.astype(BF16), route_tm)
    return peer_experts(x, xnt, u.astype(BF16), v.T.astype(BF16), r2, e2, n1, f1, final_nw, tm, eb, sub)


def _tile(n, pref):
    t = min(n, pref)
    while n % t:
        t -= LANES
    return t


def even_layer(x, nw, w_in, conv_w, w_out, batch, seq):
    t = x.shape[0]
    conv_ch = conv_w.shape[1]
    w_in = w_in.astype(BF16)
    tm = _tile(t, 1024)
    n_a = 3 * conv_ch
    n_b = w_in.shape[1] - n_a
    proj_a = norm_matmul(x, nw, w_in, 0, n_a, F32, tm, _tile(n_a, 1024))
    qkv = norm_matmul(x, nw, w_in, n_a, n_b, BF16, tm, _tile(math.gcd(n_a, n_b), 1024))
    y_conv = even_conv(proj_a, conv_w, batch, seq, _tile(conv_ch, 256))
    y_sb = sb_attention(qkv, batch, seq, _tile(seq, 256), 4)
    return matmul_resid([y_conv, y_sb], w_out.astype(BF16), x, tm, _tile(x.shape[1], 1024))


def ssd_layer(x, nw, w_in, conv_w, conv_b, dt_bias, a_log, d_skip, norm_w, w_out, batch, seq):
    t = x.shape[0]
    heads = dt_bias.shape[0]
    d_inner = heads * SSD_HEAD_DIM
    conv_dim = conv_w.shape[1]
    w_in = w_in.astype(BF16)
    tm = _tile(t, 1024)
    zx = norm_matmul(x, nw, w_in, 0, d_inner + conv_dim, F32, tm, _tile(d_inner + conv_dim, 1024))
    w_dt = jnp.pad(w_in[:, d_inner + conv_dim:], ((0, 0), (0, LANES - heads)))
    dt_raw = norm_matmul(x, nw, w_dt, 0, LANES, F32, tm, LANES)[:, :heads]
    xbc = ssd_conv(zx, conv_w, conv_b, batch, seq, d_inner, _tile(conv_dim, 512))
    y = ssd_scan(zx, xbc, dt_raw, dt_bias, a_log, d_skip, norm_w, batch, seq, 8)
    return matmul_resid([y], w_out.astype(BF16), x, tm, _tile(x.shape[1], 512))


def kernel(x, norm_mix, norm_ffn, norm_final, ev_w_in, ev_conv_w, ev_w_out, ssd_w_in, ssd_conv_w, ssd_conv_b, ssd_dt_bias, ssd_a_log, ssd_d, ssd_norm, ssd_w_out, peer_w_q, peer_sub_keys, peer_u, peer_v):
    batch, seq, d = x.shape
    t = batch * seq
    x = x.reshape(t, d)
    depth = norm_mix.shape[0]
    for i in range(depth):
        j = i // 2
        if i % 2 == 0:
            x = even_layer(x, norm_mix[i], ev_w_in[j], ev_conv_w[j], ev_w_out[j], batch, seq)
        else:
            x = ssd_layer(x, norm_mix[i], ssd_w_in[j], ssd_conv_w[j], ssd_conv_b[j], ssd_dt_bias[j],
                          ssd_a_log[j], ssd_d[j], ssd_norm[j], ssd_w_out[j], batch, seq)
        x = peer_ffn(x, norm_ffn[i], peer_w_q[i], peer_sub_keys[i], peer_u[i], peer_v[i],
                     norm_final if i == depth - 1 else None, _tile(t, 128), _tile(t, 512), 1024, 256)
    return x.reshape(batch, seq, d)
```
